```python
import jax, jax.numpy as jnp
from jax import lax
import numpy as np

D_MODEL = 1024
BATCH = 4
SEQ = 8192
DEPTH = 4

CHUNK = 64
Q_BLOCK = 128
ATTN_HEADS = 8
ATTN_HEAD_DIM = 64
D_ATTN = ATTN_HEADS * ATTN_HEAD_DIM
SSD_HEADS = 16
SSD_HEAD_DIM = 64
D_SSD = SSD_HEADS * SSD_HEAD_DIM
SSD_GROUPS = 2
SSD_STATE = 128
CONV_WIDTH = 4
CONV_CH = D_SSD + 2 * SSD_GROUPS * SSD_STATE
N_EXPERT_GROUPS = 4
EXPERTS_PER_GROUP = 8
N_EXPERTS = N_EXPERT_GROUPS * EXPERTS_PER_GROUP
TOP_K = 2
EXPERT_FF = 512
MOE_BLOCK = 256
EPS = 1e-6
IN_WIDTHS = (D_ATTN, D_ATTN, D_ATTN, ATTN_HEADS, D_SSD, CONV_CH, SSD_HEADS, D_MODEL, D_MODEL)
IN_SPLITS = tuple(sum(IN_WIDTHS[:i + 1]) for i in range(len(IN_WIDTHS) - 1))
IN_COLS = sum(IN_WIDTHS)

kernel_name = "fox_ssd_gated_hier_moe_trunk"


def rmsnorm(x, g):
    xf = x.astype(jnp.float32)
    y = xf * lax.rsqrt(jnp.mean(xf * xf, axis=-1, keepdims=True) + EPS)
    return (y * g.astype(jnp.float32)).astype(x.dtype)


def forgetting_attention(q, k, v, f_logit):
    Bb, S, H, Dh = q.shape
    log_f = jax.nn.log_sigmoid(f_logit.astype(jnp.float32))
    F = jnp.cumsum(log_f, axis=1).transpose(0, 2, 1)
    kh = k.transpose(0, 2, 1, 3)
    vh = v.transpose(0, 2, 1, 3)
    nq = S // Q_BLOCK
    q_blocks = q.transpose(0, 2, 1, 3).reshape(Bb, H, nq, Q_BLOCK, Dh).transpose(2, 0, 1, 3, 4)
    F_blocks = F.reshape(Bb, H, nq, Q_BLOCK).transpose(2, 0, 1, 3)
    key_pos = jnp.arange(S)
    scale = Dh ** -0.5

    def one_block(args):
        qb, Fq, i = args
        s = jnp.einsum("bhqd,bhkd->bhqk", qb, kh, preferred_element_type=jnp.float32) * scale
        s = s + Fq[..., None] - F[:, :, None, :]
        q_pos = i * Q_BLOCK + jnp.arange(Q_BLOCK)
        s = jnp.where(key_pos[None, :] <= q_pos[:, None], s, -jnp.inf)
        p = jax.nn.softmax(s, axis=-1)
        return jnp.einsum("bhqk,bhkd->bhqd", p.astype(vh.dtype), vh)

    out = lax.map(one_block, (q_blocks, F_blocks, jnp.arange(nq)))
    return out.transpose(1, 0, 3, 2, 4).reshape(Bb, S, H * Dh)


def causal_depthwise_conv(x, w, b):
    C = x.shape[-1]
    y = lax.conv_general_dilated(x, w[:, None, :].astype(x.dtype), window_strides=(1,),
                                 padding=[(CONV_WIDTH - 1, 0)],
                                 dimension_numbers=("NWC", "WIO", "NWC"),
                                 feature_group_count=C)
    return y + b


def ssd_chunked(xh, dt, A, Bg, Cg):
    Bb, S, H, P = xh.shape
    G, N = Bg.shape[2], Bg.shape[3]
    E = H // G
    nc, L = S // CHUNK, CHUNK
    x = (xh * dt[..., None].astype(xh.dtype)).reshape(Bb, nc, L, G, E, P)
    a = jnp.moveaxis((dt * A).reshape(Bb, nc, L, G, E), 2, -1)
    a_cs = jnp.cumsum(a, axis=-1)
    Bc = Bg.reshape(Bb, nc, L, G, N)
    Cc = Cg.reshape(Bb, nc, L, G, N)
    causal = jnp.tril(jnp.ones((L, L), dtype=bool))
    seg = a_cs[..., :, None] - a_cs[..., None, :]
    decay = jnp.exp(jnp.where(causal, seg, -jnp.inf))
    cb = jnp.einsum("bclgn,bcsgn->bcgls", Cc, Bc)
    m = cb[:, :, :, None] * decay
    y_diag = jnp.einsum("bcgels,bcsgep->bclgep", m, x)
    decay_to_end = jnp.exp(a_cs[..., -1:] - a_cs)
    chunk_states = jnp.einsum("bclgn,bcgel,bclgep->bcgepn", Bc, decay_to_end, x)
    chunk_decay = jnp.exp(a_cs[..., -1])

    def step(state, inp):
        s_c, d_c = inp
        return state * d_c[..., None, None] + s_c, state

    init = jnp.zeros((Bb, G, E, P, N), chunk_states.dtype)
    _, prev = lax.scan(step, init, (jnp.moveaxis(chunk_states, 1, 0), jnp.moveaxis(chunk_decay, 1, 0)))
    prev = jnp.moveaxis(prev, 0, 1)
    y_off = jnp.einsum("bclgn,bcgepn,bcgel->bclgep", Cc, prev, jnp.exp(a_cs))
    return (y_diag + y_off).reshape(Bb, S, H, P).astype(xh.dtype)


def hybrid_mixer(h, w_in, b_f, conv_w, conv_b, dt_bias, a_log, d_skip, ssd_norm_g,
                 w_br_attn, w_br_ssd, w_out):
    Bb, S, _ = h.shape
    proj = h @ w_in
    q, k, v, f_logit, z, xbc, dt_raw, g_attn, g_ssd = jnp.split(proj, IN_SPLITS, axis=-1)
    hs = (Bb, S, ATTN_HEADS, ATTN_HEAD_DIM)
    y_attn = forgetting_attention(q.reshape(hs), k.reshape(hs), v.reshape(hs), f_logit + b_f)
    xbc = jax.nn.silu(causal_depthwise_conv(xbc, conv_w, conv_b))
    xs, Bm, Cm = jnp.split(xbc, (D_SSD, D_SSD + SSD_GROUPS * SSD_STATE), axis=-1)
    dt = jax.nn.softplus(dt_raw.astype(jnp.float32) + dt_bias.astype(jnp.float32))
    A = -jnp.exp(a_log.astype(jnp.float32))
    xh = xs.reshape(Bb, S, SSD_HEADS, SSD_HEAD_DIM)
    y = ssd_chunked(xh, dt, A, Bm.reshape(Bb, S, SSD_GROUPS, SSD_STATE),
                    Cm.reshape(Bb, S, SSD_GROUPS, SSD_STATE))
    y = (y + xh * d_skip[:, None]).reshape(Bb, S, D_SSD)
    y_ssd = rmsnorm(y * jax.nn.silu(z), ssd_norm_g)
    merged = jax.nn.sigmoid(g_attn) * (y_attn @ w_br_attn) + jax.nn.sigmoid(g_ssd) * (y_ssd @ w_br_ssd)
    return merged @ w_out


def hierarchical_moe(h, w_gr, b_gr, w_er, b_er, w1, w3, w2):
    Bb, S, D = h.shape
    t = h.reshape(-1, D)
    T = t.shape[0]
    g_prob = jax.nn.softmax((t @ w_gr).astype(jnp.float32) + b_gr.astype(jnp.float32), axis=-1)
    g_w, g_idx = lax.top_k(g_prob, 1)
    e_logits = ((t @ w_er).astype(jnp.float32) + b_er.astype(jnp.float32)).reshape(T, N_EXPERT_GROUPS, EXPERTS_PER_GROUP)
    e_logits = jnp.take_along_axis(e_logits, g_idx[:, :, None], axis=1)[:, 0]
    e_w, e_local = lax.top_k(jax.nn.softmax(e_logits, axis=-1), TOP_K)
    expert_idx = g_idx * EXPERTS_PER_GROUP + e_local
    gate = g_w * e_w
    flat_e = expert_idx.reshape(-1)
    order = jnp.argsort(flat_e)
    sorted_e = flat_e[order]
    tok = order // TOP_K
    counts = jnp.zeros((N_EXPERTS,), jnp.int32).at[flat_e].add(1)
    padded = (counts + MOE_BLOCK - 1) // MOE_BLOCK * MOE_BLOCK
    start = jnp.cumsum(counts) - counts
    pad_start = jnp.cumsum(padded) - padded
    dest = pad_start[sorted_e] + jnp.arange(T * TOP_K) - start[sorted_e]
    n_blocks = -(-(T * TOP_K) // MOE_BLOCK) + N_EXPERTS
    buf = jnp.zeros((n_blocks * MOE_BLOCK, D), t.dtype).at[dest].set(t[tok])
    block_end = jnp.cumsum(padded) // MOE_BLOCK
    block_expert = jnp.minimum(jnp.searchsorted(block_end, jnp.arange(n_blocks), side="right"), N_EXPERTS - 1)

    def expert_block(args):
        xb, e = args
        return (jax.nn.silu(xb @ w1[e]) * (xb @ w3[e])) @ w2[e]

    out = lax.map(expert_block, (buf.reshape(n_blocks, MOE_BLOCK, D), block_expert)).reshape(-1, D)
    y_sorted = out[dest] * gate.reshape(-1)[order][:, None].astype(out.dtype)
    y = jnp.zeros((T * TOP_K, D), out.dtype).at[order].set(y_sorted).reshape(T, TOP_K, D).sum(axis=1)
    return y.reshape(Bb, S, D)


def setup_inputs(seed: int = 0) -> dict:
    key = jax.random.key(seed)
    ks = jax.random.split(key, 24)
    f32 = jnp.float32
    nrm = lambda k, shape, fan: jax.random.normal(k, shape, f32) * fan ** -0.5
    dt0 = jnp.exp(jax.random.uniform(ks[7], (DEPTH, SSD_HEADS), f32) * (np.log(0.1) - np.log(0.001)) + np.log(0.001))
    return {
        "x": jax.random.normal(ks[0], (BATCH, SEQ, D_MODEL), f32),
        "norm_mix_g": 1.0 + 0.05 * jax.random.normal(ks[1], (DEPTH, D_MODEL), f32),
        "w_in": nrm(ks[2], (DEPTH, D_MODEL, IN_COLS), D_MODEL),
        "b_f": jax.random.uniform(ks[3], (DEPTH, ATTN_HEADS), f32, 1.0, 6.0),
        "conv_w": nrm(ks[4], (DEPTH, CONV_WIDTH, CONV_CH), CONV_WIDTH),
        "conv_b": 0.02 * jax.random.normal(ks[5], (DEPTH, CONV_CH), f32),
        "dt_bias": dt0 + jnp.log(-jnp.expm1(-dt0)),
        "a_log": jnp.log(jax.random.uniform(ks[8], (DEPTH, SSD_HEADS), f32, 1.0, 16.0)),
        "d_skip": 1.0 + 0.1 * jax.random.normal(ks[9], (DEPTH, SSD_HEADS), f32),
        "ssd_norm_g": 1.0 + 0.05 * jax.random.normal(ks[10], (DEPTH, D_SSD), f32),
        "w_br_attn": nrm(ks[11], (DEPTH, D_ATTN, D_MODEL), D_ATTN),
        "w_br_ssd": nrm(ks[12], (DEPTH, D_SSD, D_MODEL), D_SSD),
        "w_out": nrm(ks[13], (DEPTH, D_MODEL, D_MODEL), D_MODEL),
        "norm_ffn_g": 1.0 + 0.05 * jax.random.normal(ks[14], (DEPTH, D_MODEL), f32),
        "w_group_router": nrm(ks[15], (DEPTH, D_MODEL, N_EXPERT_GROUPS), D_MODEL),
        "b_group_router": 0.01 * jax.random.normal(ks[16], (DEPTH, N_EXPERT_GROUPS), f32),
        "w_expert_router": nrm(ks[17], (DEPTH, D_MODEL, N_EXPERTS), D_MODEL),
        "b_expert_router": 0.01 * jax.random.normal(ks[18], (DEPTH, N_EXPERTS), f32),
        "w1": nrm(ks[19], (DEPTH, N_EXPERTS, D_MODEL, EXPERT_FF), D_MODEL),
        "w3": nrm(ks[20], (DEPTH, N_EXPERTS, D_MODEL, EXPERT_FF), D_MODEL),
        "w2": nrm(ks[21], (DEPTH, N_EXPERTS, EXPERT_FF, D_MODEL), EXPERT_FF),
        "final_g": 1.0 + 0.05 * jax.random.normal(ks[22], (D_MODEL,), f32),
    }


def reference(x, norm_mix_g, w_in, b_f, conv_w, conv_b, dt_bias, a_log, d_skip, ssd_norm_g,
              w_br_attn, w_br_ssd, w_out, norm_ffn_g, w_group_router, b_group_router,
              w_expert_router, b_expert_router, w1, w3, w2, final_g):
    for l in range(DEPTH):
        h = rmsnorm(x, norm_mix_g[l])
        x = x + hybrid_mixer(h, w_in[l], b_f[l], conv_w[l], conv_b[l], dt_bias[l], a_log[l],
                             d_skip[l], ssd_norm_g[l], w_br_attn[l], w_br_ssd[l], w_out[l])
        h = rmsnorm(x, norm_ffn_g[l])
        x = x + hierarchical_moe(h, w_group_router[l], b_group_router[l], w_expert_router[l],
                                 b_expert_router[l], w1[l], w3[l], w2[l])
    return rmsnorm(x, final_g)
```

```python
import functools

import numpy as np
import jax
import jax.numpy as jnp
from jax import lax
from jax.experimental import pallas as pl
from jax.experimental.pallas import tpu as pltpu

F32 = jnp.float32
BF16 = jnp.bfloat16

ATTN_HEADS = 8
ATTN_HEAD_DIM = 64
D_ATTN = ATTN_HEADS * ATTN_HEAD_DIM
SSD_HEADS = 16
SSD_HEAD_DIM = 64
D_SSD = SSD_HEADS * SSD_HEAD_DIM
SSD_GROUPS = 2
SSD_STATE = 128
CONV_WIDTH = 4
CONV_CH = D_SSD + 2 * SSD_GROUPS * SSD_STATE
N_EXPERT_GROUPS = 4
EXPERTS_PER_GROUP = 8
N_EXPERTS = N_EXPERT_GROUPS * EXPERTS_PER_GROUP
TOP_K = 2
EPS = 1e-6

LANES = 128
HEAD_LANE0 = 8
SMALL_W = LANES
NEG_BIG = -1e30
VMEM_LIMIT = 56 * 1024 * 1024

SSD_CHUNK = 128
ATTN_TILE = 512
FCUM_TILE = 512
ROW_TILE = 512
MOE_BLOCK = 256


def _cparams(sem):
    return pltpu.CompilerParams(dimension_semantics=sem, vmem_limit_bytes=VMEM_LIMIT)


def _split3(x):
    hi = x.astype(BF16)
    r = x - hi.astype(F32)
    mid = r.astype(BF16)
    lo = (r - mid.astype(F32)).astype(BF16)
    return hi, mid, lo


def _dot(a, b):
    return jnp.dot(a, b, preferred_element_type=F32)


def _dot_nt(a, b):
    return lax.dot_general(a, b, (((1,), (1,)), ((), ())), preferred_element_type=F32)


def _dot_x_01(x, m01):
    hi, mid, lo = _split3(x)
    return _dot(hi, m01) + _dot(mid, m01) + _dot(lo, m01)


def _dot_01_x(m01, x):
    hi, mid, lo = _split3(x)
    return _dot(m01, hi) + _dot(m01, mid) + _dot(m01, lo)


def _tril01(n):
    r = lax.broadcasted_iota(jnp.int32, (n, n), 0)
    c = lax.broadcasted_iota(jnp.int32, (n, n), 1)
    return (c <= r).astype(BF16)


def _softplus(x):
    return jnp.maximum(x, 0.0) + jnp.log1p(jnp.exp(-jnp.abs(x)))


def _sigmoid(x):
    return 1.0 / (1.0 + jnp.exp(-x))


def _inproj_kernel(x_ref, g_ref, w_ref, qkv_ref, z_ref, xbc_ref, gates_ref, small_ref):
    x = x_ref[...]
    ms = jnp.mean(x * x, axis=-1, keepdims=True)
    h = (x * lax.rsqrt(ms + EPS) * g_ref[...]).astype(BF16)
    off = 0
    for ref in (qkv_ref, z_ref, xbc_ref, gates_ref, small_ref):
        n = ref.shape[-1]
        for c0 in range(0, n, 512):
            c1 = min(c0 + 512, n)
            ref[:, c0:c1] = _dot(h, w_ref[:, off + c0:off + c1]).astype(ref.dtype)
        off += n


def _inproj(x2, g, w_packed):
    T, D = x2.shape
    widths = (3 * D_ATTN, D_SSD, CONV_CH, 2 * D, SMALL_W)
    dtypes = (BF16, F32, F32, F32, F32)
    tm = min(ROW_TILE, T)
    return pl.pallas_call(
        _inproj_kernel,
        grid=(T // tm,),
        in_specs=[
            pl.BlockSpec((tm, D), lambda i: (i, 0)),
            pl.BlockSpec((1, D), lambda i: (0, 0)),
            pl.BlockSpec(w_packed.shape, lambda i: (0, 0), pipeline_mode=pl.Buffered(1)),
        ],
        out_specs=[pl.BlockSpec((tm, n), lambda i: (i, 0)) for n in widths],
        out_shape=[jax.ShapeDtypeStruct((T, n), dt) for n, dt in zip(widths, dtypes)],
        compiler_params=_cparams(("parallel",)),
        name="inproj",
    )(x2, g, w_packed)


def _fcum_kernel(s_ref, b_ref, frow_ref, carry_ref):
    c = pl.program_id(1)

    @pl.when(c == 0)
    def _():
        carry_ref[...] = jnp.zeros_like(carry_ref)

    x = s_ref[...] + b_ref[...]
    logf = jnp.minimum(x, 0.0) - jnp.log1p(jnp.exp(-jnp.abs(x)))
    n = x.shape[0]
    cum = _dot_01_x(_tril01(n), logf) + carry_ref[0:1, :]
    carry_ref[0:1, :] = cum[n - 1:n, :]
    frow_ref[...] = cum.T[0:ATTN_HEADS, :]


def _fcum(small3, bf_row):
    B, S, _ = small3.shape
    L = min(FCUM_TILE, S)
    return pl.pallas_call(
        _fcum_kernel,
        grid=(B, S // L),
        in_specs=[
            pl.BlockSpec((None, L, SMALL_W), lambda b, c: (b, c, 0)),
            pl.BlockSpec((1, SMALL_W), lambda b, c: (0, 0)),
        ],
        out_specs=pl.BlockSpec((None, ATTN_HEADS, L), lambda b, c: (b, 0, c)),
        out_shape=jax.ShapeDtypeStruct((B, ATTN_HEADS, S), F32),
        scratch_shapes=[pltpu.VMEM((8, SMALL_W), F32)],
        compiler_params=_cparams(("parallel", "arbitrary")),
        name="forget_cumsum",
    )(small3, bf_row)


def _attn_kernel(q_ref, k_ref, v_ref, frow_ref, o_ref, vlo_ref, vhi_ref, *, t):
    hp = pl.program_id(1)
    i = pl.program_id(2)
    lo = lax.broadcasted_iota(jnp.int32, (1, LANES), 1) < ATTN_HEAD_DIM

    @pl.when(i == 0)
    def _():
        v = v_ref[...]
        zero = jnp.zeros_like(v)
        vlo_ref[...] = jnp.where(lo, v, zero)
        vhi_ref[...] = jnp.where(lo, zero, v)

    q = q_ref[...]
    qz = jnp.zeros_like(q)
    q_heads = (jnp.where(lo, q, qz), jnp.where(lo, qz, q))
    v_refs = (vlo_ref, vhi_ref)

    def step(kb, carry, masked):
        ks = pl.multiple_of(kb * t, t)
        k = k_ref[pl.ds(ks, t), :]
        acc = carry[-1]
        new = []
        pv = None
        alphas = []
        for e in range(2):
            m_prev, l_prev = carry[2 * e], carry[2 * e + 1]
            fk = frow_ref[pl.ds(2 * hp + e, 1), pl.ds(ks, t)]
            s = _dot_nt(q_heads[e], k) - fk
            if masked:
                r = lax.broadcasted_iota(jnp.int32, (t, t), 0)
                c = lax.broadcasted_iota(jnp.int32, (t, t), 1)
                s = jnp.where(c <= r, s, NEG_BIG)
            m_new = jnp.maximum(m_prev, jnp.max(s, axis=-1, keepdims=True))
            alpha = jnp.exp(m_prev - m_new)
            p = jnp.exp(s - m_new)
            l_new = alpha * l_prev + jnp.sum(p, axis=-1, keepdims=True)
            d = _dot(p.astype(BF16), v_refs[e][pl.ds(ks, t), :])
            pv = d if pv is None else pv + d
            alphas.append(alpha)
            new += [m_new, l_new]
        acc = acc * jnp.where(lo, alphas[0], alphas[1]) + pv
        return tuple(new) + (acc,)

    init = (jnp.full((t, 1), NEG_BIG, F32), jnp.zeros((t, 1), F32),
            jnp.full((t, 1), NEG_BIG, F32), jnp.zeros((t, 1), F32),
            jnp.zeros((t, LANES), F32))
    carry = lax.fori_loop(0, i, lambda kb, c: step(kb, c, False), init)
    _, la, _, lb, acc = step(i, carry, True)
    o_ref[...] = (acc * jnp.where(lo, 1.0 / la, 1.0 / lb)).astype(o_ref.dtype)


def _attention(qkv3, frow):
    B, S, _ = qkv3.shape
    t = min(ATTN_TILE, S)
    npairs = D_ATTN // LANES
    return pl.pallas_call(
        functools.partial(_attn_kernel, t=t),
        grid=(B, npairs, S // t),
        in_specs=[
            pl.BlockSpec((None, t, LANES), lambda b, h, i: (b, i, h)),
            pl.BlockSpec((None, S, LANES), lambda b, h, i: (b, 0, npairs + h)),
            pl.BlockSpec((None, S, LANES), lambda b, h, i: (b, 0, 2 * npairs + h)),
            pl.BlockSpec((None, ATTN_HEADS, S), lambda b, h, i: (b, 0, 0)),
        ],
        out_specs=pl.BlockSpec((None, t, LANES), lambda b, h, i: (b, i, h)),
        out_shape=jax.ShapeDtypeStruct((B, S, D_ATTN), BF16),
        scratch_shapes=[pltpu.VMEM((S, LANES), BF16), pltpu.VMEM((S, LANES), BF16)],
        compiler_params=_cparams(("parallel", "parallel", "arbitrary")),
        name="fox_attention",
    )(qkv3, qkv3, qkv3, frow)


def _ssd_kernel(xbc_ref, z_ref, small_ref, cw_ref, cb_ref, dtb_ref, alog_ref, dskip_ref, ng_ref,
                e1_ref, e2_ref, o_ref, xbuf_ref, state_ref, *, L):
    c = pl.program_id(1)
    halo = 8

    @pl.when(c == 0)
    def _():
        xbuf_ref[0:halo, :] = jnp.zeros((halo, CONV_CH), F32)
        state_ref[...] = jnp.zeros_like(state_ref)

    @pl.when(c > 0)
    def _():
        xbuf_ref[0:halo, :] = xbuf_ref[L:L + halo, :]

    xbuf_ref[halo:halo + L, :] = xbc_ref[...]
    w = cw_ref[...]
    conv = cb_ref[...] + w[3:4, :] * xbuf_ref[halo:halo + L, :]
    for kk in range(CONV_WIDTH - 1):
        conv = conv + w[kk:kk + 1, :] * xbuf_ref[pl.ds(halo - (CONV_WIDTH - 1) + kk, L), :]
    u = conv * _sigmoid(conv)
    xs = u[:, :D_SSD]
    bm = u[:, D_SSD:D_SSD + SSD_GROUPS * SSD_STATE]
    cm = u[:, D_SSD + SSD_GROUPS * SSD_STATE:]

    dt = _softplus(small_ref[...] + dtb_ref[...])
    a = dt * (-jnp.exp(alog_ref[...]))
    acs = _dot_01_x(_tril01(L), a)
    e1 = e1_ref[...]
    dt_e = _dot_x_01(dt, e1)
    acs_e = _dot_x_01(acs, e1)
    acs_b = _dot_x_01(acs, e2_ref[...])
    acs_t = acs.T
    last_e = acs_e[L - 1:L, :]
    x_dt = xs * dt_e
    x_dt_bf = x_dt.astype(BF16)
    x_end_bf = (x_dt * jnp.exp(last_e - acs_e)).astype(BF16)
    grow = jnp.exp(acs_e)
    chunk_decay = jnp.exp(last_e)

    row = lax.broadcasted_iota(jnp.int32, (L, L), 0)
    col = lax.broadcasted_iota(jnp.int32, (L, L), 1)
    causal = col <= row
    lo = lax.broadcasted_iota(jnp.int32, (1, LANES), 1) < SSD_HEAD_DIM
    pairs_per_group = SSD_HEADS // SSD_GROUPS // 2

    ys = []
    for g in range(SSD_GROUPS):
        bg = bm[:, g * SSD_STATE:(g + 1) * SSD_STATE]
        bg_bf = bg.astype(BF16)
        bg_t_bf = bg.T.astype(BF16)
        cg_bf = cm[:, g * SSD_STATE:(g + 1) * SSD_STATE].astype(BF16)
        cb = _dot_nt(cg_bf, bg_bf)
        for jj in range(pairs_per_group):
            j = g * pairs_per_group + jj
            sl = slice(j * LANES, (j + 1) * LANES)
            xp = x_dt_bf[:, sl]
            yd = []
            for e in range(2):
                h = 2 * j + e
                seg = acs_b[:, h * L:(h + 1) * L] - acs_t[HEAD_LANE0 + h:HEAD_LANE0 + h + 1, :]
                dec = jnp.exp(jnp.where(causal, seg, NEG_BIG))
                yd.append(_dot((cb * dec).astype(BF16), xp))
            st = state_ref[j]
            y_off = _dot(cg_bf, st.astype(BF16)) * grow[:, sl]
            ys.append(jnp.where(lo, yd[0], yd[1]) + y_off)
            state_ref[j] = st * chunk_decay[:, sl] + _dot(bg_t_bf, x_end_bf[:, sl])

    y = jnp.concatenate(ys, axis=1) + xs * dskip_ref[...]
    zz = z_ref[...]
    y = y * (zz * _sigmoid(zz))
    ms = jnp.mean(y * y, axis=-1, keepdims=True)
    o_ref[...] = (y * lax.rsqrt(ms + EPS) * ng_ref[...]).astype(o_ref.dtype)


def _expand_mats(L):
    e1 = np.zeros((LANES, D_SSD), np.float32)
    e2 = np.zeros((LANES, SSD_HEADS * L), np.float32)
    for h in range(SSD_HEADS):
        e1[HEAD_LANE0 + h, h * SSD_HEAD_DIM:(h + 1) * SSD_HEAD_DIM] = 1.0
        e2[HEAD_LANE0 + h, h * L:(h + 1) * L] = 1.0
    return jnp.asarray(e1, BF16), jnp.asarray(e2, BF16)


def _ssd(xbc3, z3, small3, conv_w, conv_b, dtb_row, alog_row, dskip_row, norm_g):
    B, S, _ = xbc3.shape
    L = min(SSD_CHUNK, S)
    e1, e2 = _expand_mats(L)
    const = lambda shape: pl.BlockSpec(shape, lambda b, c: (0,) * len(shape))
    return pl.pallas_call(
        functools.partial(_ssd_kernel, L=L),
        grid=(B, S // L),
        in_specs=[
            pl.BlockSpec((None, L, CONV_CH), lambda b, c: (b, c, 0)),
            pl.BlockSpec((None, L, D_SSD), lambda b, c: (b, c, 0)),
            pl.BlockSpec((None, L, SMALL_W), lambda b, c: (b, c, 0)),
            const((CONV_WIDTH, CONV_CH)), const((1, CONV_CH)), const((1, SMALL_W)), const((1, SMALL_W)),
            const((1, D_SSD)), const((1, D_SSD)), const(e1.shape), const(e2.shape),
        ],
        out_specs=pl.BlockSpec((None, L, D_SSD), lambda b, c: (b, c, 0)),
        out_shape=jax.ShapeDtypeStruct((B, S, D_SSD), BF16),
        scratch_shapes=[pltpu.VMEM((L + 8, CONV_CH), F32),
                        pltpu.VMEM((SSD_HEADS // 2, SSD_STATE, LANES), F32)],
        compiler_params=_cparams(("parallel", "arbitrary")),
        name="conv_ssd",
    )(xbc3, z3, small3, conv_w, conv_b, dtb_row, alog_row, dskip_row, norm_g, e1, e2)


def _merge_kernel(ya_ref, ys_ref, gates_ref, x_ref, wa_ref, ws_ref, wo_ref, g2_ref, wr_ref, br_ref,
                  xo_ref, h_ref, lg_ref):
    D = x_ref.shape[-1]
    pa = _dot(ya_ref[...], wa_ref[...])
    ps = _dot(ys_ref[...], ws_ref[...])
    merged = _sigmoid(gates_ref[:, :D]) * pa + _sigmoid(gates_ref[:, D:]) * ps
    xn = x_ref[...] + _dot(merged.astype(BF16), wo_ref[...])
    xo_ref[...] = xn
    ms = jnp.mean(xn * xn, axis=-1, keepdims=True)
    h = xn * lax.rsqrt(ms + EPS) * g2_ref[...]
    h_ref[...] = h.astype(BF16)
    h_hi, h_mid, _ = _split3(h)
    w_hi, w_mid, _ = _split3(wr_ref[...])
    lg_ref[...] = _dot(h_hi, w_hi) + _dot(h_hi, w_mid) + _dot(h_mid, w_hi) + br_ref[...]


def _merge(ya, ys, gates, x2, wa, ws, wo, g2, wr, br):
    T, D = x2.shape
    tm = min(ROW_TILE, T)
    row = lambda n: pl.BlockSpec((tm, n), lambda i: (i, 0))
    const = lambda a: pl.BlockSpec(a.shape, lambda i: (0, 0))
    return pl.pallas_call(
        _merge_kernel,
        grid=(T // tm,),
        in_specs=[row(D_ATTN), row(D_SSD), row(2 * D), row(D),
                  const(wa), const(ws), const(wo), const(g2), const(wr), const(br)],
        out_specs=[row(D), row(D), row(LANES)],
        out_shape=[jax.ShapeDtypeStruct((T, D), F32), jax.ShapeDtypeStruct((T, D), BF16),
                   jax.ShapeDtypeStruct((T, LANES), F32)],
        compiler_params=_cparams(("parallel",)),
        name="merge_outproj_router",
    )(ya, ys, gates, x2, wa, ws, wo, g2, wr, br)


def _expert_kernel(be_ref, nu_ref, x_ref, w1_ref, w3_ref, w2_ref, o_ref):
    @pl.when(pl.program_id(0) < nu_ref[0])
    def _():
        x = x_ref[...]
        h1 = _dot(x, w1_ref[...])
        h3 = _dot(x, w3_ref[...])
        a = (h1 * _sigmoid(h1) * h3).astype(BF16)
        o_ref[...] = _dot(a, w2_ref[...])


def _experts(buf, block_expert, n_used, w1, w3, w2):
    R, D = buf.shape
    FF = w1.shape[-1]
    nb = R // MOE_BLOCK
    blk = lambda i, be, nu: (jnp.minimum(i, nu[0] - 1), 0)
    wsel = lambda i, be, nu: (be[jnp.minimum(i, nu[0] - 1)], 0, 0)
    return pl.pallas_call(
        _expert_kernel,
        grid_spec=pltpu.PrefetchScalarGridSpec(
            num_scalar_prefetch=2,
            grid=(nb,),
            in_specs=[
                pl.BlockSpec((MOE_BLOCK, D), blk),
                pl.BlockSpec((None, D, FF), wsel),
                pl.BlockSpec((None, D, FF), wsel),
                pl.BlockSpec((None, FF, D), wsel),
            ],
            out_specs=pl.BlockSpec((MOE_BLOCK, D), blk),
        ),
        out_shape=jax.ShapeDtypeStruct((R, D), F32),
        compiler_params=_cparams(("arbitrary",)),
        name="expert_mlp",
    )(block_expert, n_used, buf, w1, w3, w2)


def _norm_kernel(x_ref, g_ref, o_ref):
    x = x_ref[...]
    ms = jnp.mean(x * x, axis=-1, keepdims=True)
    o_ref[...] = x * lax.rsqrt(ms + EPS) * g_ref[...]


def _final_norm(x2, g):
    T, D = x2.shape
    tm = min(ROW_TILE, T)
    return pl.pallas_call(
        _norm_kernel,
        grid=(T // tm,),
        in_specs=[pl.BlockSpec((tm, D), lambda i: (i, 0)), pl.BlockSpec((1, D), lambda i: (0, 0))],
        out_specs=pl.BlockSpec((tm, D), lambda i: (i, 0)),
        out_shape=jax.ShapeDtypeStruct((T, D), F32),
        compiler_params=_cparams(("parallel",)),
        name="final_norm",
    )(x2, g)


def _pack_w_in(w_in):
    D = w_in.shape[0]
    widths = (D_ATTN, D_ATTN, D_ATTN, ATTN_HEADS, D_SSD, CONV_CH, SSD_HEADS, D, D)
    offs = np.concatenate([[0], np.cumsum(widths)])
    q, k, v, f, z, xbc, dt, ga, gs = (w_in[:, offs[i]:offs[i + 1]] for i in range(len(widths)))
    pad = jnp.zeros((D, SMALL_W - ATTN_HEADS - SSD_HEADS), w_in.dtype)
    scale = ATTN_HEAD_DIM ** -0.5
    return jnp.concatenate([q * scale, k, v, z, xbc, ga, gs, f, dt, pad], axis=1).astype(BF16)


def _small_row(vec, lane0):
    return jnp.zeros((1, SMALL_W), F32).at[0, lane0:lane0 + vec.shape[0]].set(vec.astype(F32))


def _route(logits):
    T = logits.shape[0]
    g_prob = jax.nn.softmax(logits[:, :N_EXPERT_GROUPS], axis=-1)
    g_w, g_idx = lax.top_k(g_prob, 1)
    e_logits = logits[:, N_EXPERT_GROUPS:N_EXPERT_GROUPS + N_EXPERTS].reshape(T, N_EXPERT_GROUPS, EXPERTS_PER_GROUP)
    e_logits = jnp.take_along_axis(e_logits, g_idx[:, :, None], axis=1)[:, 0]
    e_w, e_local = lax.top_k(jax.nn.softmax(e_logits, axis=-1), TOP_K)
    return g_idx * EXPERTS_PER_GROUP + e_local, g_w * e_w


def _dispatch_plan(expert_idx):
    T = expert_idx.shape[0]
    flat_e = expert_idx.reshape(-1)
    onehot = (flat_e[:, None] == jnp.arange(N_EXPERTS)[None, :]).astype(jnp.int32)
    ranks = jnp.cumsum(onehot, axis=0)
    counts = ranks[-1]
    rank = jnp.take_along_axis(ranks, flat_e[:, None], axis=1)[:, 0] - 1
    padded = (counts + MOE_BLOCK - 1) // MOE_BLOCK * MOE_BLOCK
    pad_end = jnp.cumsum(padded)
    dest = (pad_end - padded)[flat_e] + rank
    n_blocks = -(-(T * TOP_K) // MOE_BLOCK) + N_EXPERTS
    block_end = pad_end // MOE_BLOCK
    block_expert = jnp.minimum(jnp.searchsorted(block_end, jnp.arange(n_blocks), side="right"), N_EXPERTS - 1)
    n_used = block_end[-1:].astype(jnp.int32)
    src_tok = jnp.zeros((n_blocks * MOE_BLOCK,), jnp.int32).at[dest].set(jnp.arange(T * TOP_K, dtype=jnp.int32) // TOP_K)
    return dest, src_tok, block_expert.astype(jnp.int32), n_used


def kernel(x, norm_mix_g, w_in, b_f, conv_w, conv_b, dt_bias, a_log, d_skip, ssd_norm_g, w_br_attn, w_br_ssd,
           w_out, norm_ffn_g, w_group_router, b_group_router, w_expert_router, b_expert_router, w1, w3, w2,
           final_g):
    B, S, D = x.shape
    T = B * S
    depth = w_in.shape[0]
    x2 = x.reshape(T, D)
    for l in range(depth):
        qkv, z, xbc, gates, small = _inproj(x2, norm_mix_g[l][None, :], _pack_w_in(w_in[l]))
        small3 = small.reshape(B, S, SMALL_W)
        frow = _fcum(small3, _small_row(b_f[l], 0))
        y_attn = _attention(qkv.reshape(B, S, 3 * D_ATTN), frow).reshape(T, D_ATTN)
        y_ssd = _ssd(xbc.reshape(B, S, CONV_CH), z.reshape(B, S, D_SSD), small3, conv_w[l], conv_b[l][None, :],
                     _small_row(dt_bias[l], HEAD_LANE0), _small_row(a_log[l], HEAD_LANE0),
                     jnp.repeat(d_skip[l], SSD_HEAD_DIM)[None, :], ssd_norm_g[l][None, :]).reshape(T, D_SSD)
        w_router = jnp.zeros((D, LANES), F32).at[:, :N_EXPERT_GROUPS].set(w_group_router[l]) \
            .at[:, N_EXPERT_GROUPS:N_EXPERT_GROUPS + N_EXPERTS].set(w_expert_router[l])
        b_router = jnp.zeros((1, LANES), F32).at[0, :N_EXPERT_GROUPS].set(b_group_router[l]) \
            .at[0, N_EXPERT_GROUPS:N_EXPERT_GROUPS + N_EXPERTS].set(b_expert_router[l])
        x2, h2, logits = _merge(y_attn, y_ssd, gates, x2, w_br_attn[l].astype(BF16), w_br_ssd[l].astype(BF16),
                                w_out[l].astype(BF16), norm_ffn_g[l][None, :], w_router, b_router)
        expert_idx, gate = _route(logits)
        dest, src_tok, block_expert, n_used = _dispatch_plan(expert_idx)
        out = _experts(h2[src_tok], block_expert, n_used, w1[l].astype(BF16), w3[l].astype(BF16),
                       w2[l].astype(BF16))
        y = (out[dest].reshape(T, TOP_K, D) * gate[:, :, None]).sum(axis=1)
        x2 = x2 + y
    return _final_norm(x2, final_g[None, :]).reshape(B, S, D)
```

```python
import functools

import numpy as np
import jax
import jax.numpy as jnp
from jax import lax
from jax.experimental import pallas as pl
from jax.experimental.pallas import tpu as pltpu

F32 = jnp.float32
BF16 = jnp.bfloat16

ATTN_HEADS = 8
ATTN_HEAD_DIM = 64
D_ATTN = ATTN_HEADS * ATTN_HEAD_DIM
SSD_HEADS = 16
SSD_HEAD_DIM = 64
D_SSD = SSD_HEADS * SSD_HEAD_DIM
SSD_GROUPS = 2
SSD_STATE = 128
CONV_WIDTH = 4
CONV_CH = D_SSD + 2 * SSD_GROUPS * SSD_STATE
N_EXPERT_GROUPS = 4
EXPERTS_PER_GROUP = 8
N_EXPERTS = N_EXPERT_GROUPS * EXPERTS_PER_GROUP
TOP_K = 2
EPS = 1e-6

LANES = 128
HEAD_LANE0 = 8
SMALL_W = LANES
NEG_BIG = -1e30
LOG2E = 1.4426950408889634
VMEM_LIMIT = 56 * 1024 * 1024

SSD_CHUNK = 128
ATTN_TILE = 512
FCUM_TILE = 512
ROW_TILE = 512
MOE_BLOCK = 256
R_E0, R_E1, R_G0, R_G1, R_RANK0, R_RANK1 = range(6)


def _cparams(sem):
    return pltpu.CompilerParams(dimension_semantics=sem, vmem_limit_bytes=VMEM_LIMIT)


def _split3(x):
    hi = x.astype(BF16)
    r = x - hi.astype(F32)
    mid = r.astype(BF16)
    lo = (r - mid.astype(F32)).astype(BF16)
    return hi, mid, lo


def _dot(a, b):
    return jnp.dot(a, b, preferred_element_type=F32)


def _dot_nt(a, b):
    return lax.dot_general(a, b, (((1,), (1,)), ((), ())), preferred_element_type=F32)


def _dot_x_01(x, m01):
    hi, mid, lo = _split3(x)
    return _dot(hi, m01) + _dot(mid, m01) + _dot(lo, m01)


def _dot_01_x(m01, x):
    hi, mid, lo = _split3(x)
    return _dot(m01, hi) + _dot(m01, mid) + _dot(m01, lo)


def _tril01(n, strict=False):
    r = lax.broadcasted_iota(jnp.int32, (n, n), 0)
    c = lax.broadcasted_iota(jnp.int32, (n, n), 1)
    return ((c < r) if strict else (c <= r)).astype(BF16)


def _softplus(x):
    return jnp.maximum(x, 0.0) + jnp.log1p(jnp.exp(-jnp.abs(x)))


def _sigmoid(x):
    return 1.0 / (1.0 + jnp.exp(-x))


def _rms(x, g):
    ms = jnp.mean(x * x, axis=-1, keepdims=True)
    return x * lax.rsqrt(ms + EPS) * g


def _moe_combine(x_ref, r0_ref, r1_ref, rt_ref):
    rt = rt_ref[...]
    return x_ref[...] + (rt[:, R_G0:R_G0 + 1] * r0_ref[...] + rt[:, R_G1:R_G1 + 1] * r1_ref[...])


def _inproj_kernel(*refs, with_moe):
    if with_moe:
        x_ref, r0_ref, r1_ref, rt_ref, g_ref, w_ref, xo_ref, *outs = refs
        x = _moe_combine(x_ref, r0_ref, r1_ref, rt_ref)
        xo_ref[...] = x
    else:
        x_ref, g_ref, w_ref, *outs = refs
        x = x_ref[...]
    h = _rms(x, g_ref[...]).astype(BF16)
    off = 0
    for ref in outs:
        n = ref.shape[-1]
        for c0 in range(0, n, 512):
            c1 = min(c0 + 512, n)
            ref[:, c0:c1] = _dot(h, w_ref[:, off + c0:off + c1]).astype(ref.dtype)
        off += n


def _inproj(x2, moe, g, w_packed):
    T, D = x2.shape
    widths = (3 * D_ATTN, D_SSD, CONV_CH, 2 * D, SMALL_W)
    dtypes = (BF16, F32, F32, F32, F32)
    tm = min(ROW_TILE, T)
    row = lambda n: pl.BlockSpec((tm, n), lambda i: (i, 0))
    with_moe = moe is not None
    ins = [x2] + (list(moe) if with_moe else []) + [g, w_packed]
    in_specs = [row(D)] + ([row(D), row(D), row(LANES)] if with_moe else []) + [
        pl.BlockSpec((1, D), lambda i: (0, 0)),
        pl.BlockSpec(w_packed.shape, lambda i: (0, 0), pipeline_mode=pl.Buffered(1)),
    ]
    out_specs = ([row(D)] if with_moe else []) + [row(n) for n in widths]
    out_shape = ([jax.ShapeDtypeStruct((T, D), F32)] if with_moe else []) + [
        jax.ShapeDtypeStruct((T, n), dt) for n, dt in zip(widths, dtypes)]
    res = pl.pallas_call(
        functools.partial(_inproj_kernel, with_moe=with_moe),
        grid=(T // tm,),
        in_specs=in_specs,
        out_specs=out_specs,
        out_shape=out_shape,
        compiler_params=_cparams(("parallel",)),
        name="inproj",
    )(*ins)
    return (res[0], res[1:]) if with_moe else (x2, res)


N_FPARTS = 3
HEAD_PAIRS = D_ATTN // LANES


def _fcum_kernel(s_ref, b_ref, place_ref, fk_ref, carry_ref):
    c = pl.program_id(1)

    @pl.when(c == 0)
    def _():
        carry_ref[...] = jnp.zeros_like(carry_ref)

    x = s_ref[...] + b_ref[...]
    logf = jnp.minimum(x, 0.0) - jnp.log1p(jnp.exp(-jnp.abs(x)))
    n = x.shape[0]
    cum = _dot_01_x(_tril01(n), logf) + carry_ref[0:1, :]
    carry_ref[0:1, :] = cum[n - 1:n, :]
    parts = jnp.concatenate(_split3(cum * (-LOG2E)), axis=1)
    for p in range(HEAD_PAIRS):
        fk_ref[p] = _dot(parts, place_ref[p]).astype(BF16)


def _fpart_placement():
    m = np.zeros((HEAD_PAIRS, N_FPARTS * LANES, LANES), np.float32)
    for p in range(HEAD_PAIRS):
        for j in range(N_FPARTS):
            m[p, j * LANES + 2 * p, ATTN_HEAD_DIM + j] = 1.0
            m[p, j * LANES + 2 * p + 1, j] = 1.0
    return jnp.asarray(m, BF16)


def _fcum(small3, bf_row):
    B, S, _ = small3.shape
    L = min(FCUM_TILE, S)
    place = _fpart_placement()
    return pl.pallas_call(
        _fcum_kernel,
        grid=(B, S // L),
        in_specs=[
            pl.BlockSpec((None, L, SMALL_W), lambda b, c: (b, c, 0)),
            pl.BlockSpec((1, SMALL_W), lambda b, c: (0, 0)),
            pl.BlockSpec(place.shape, lambda b, c: (0, 0, 0)),
        ],
        out_specs=pl.BlockSpec((None, HEAD_PAIRS, L, LANES), lambda b, c: (b, 0, c, 0)),
        out_shape=jax.ShapeDtypeStruct((B, HEAD_PAIRS, S, LANES), BF16),
        scratch_shapes=[pltpu.VMEM((8, SMALL_W), F32)],
        compiler_params=_cparams(("parallel", "arbitrary")),
        name="forget_cumsum",
    )(small3, bf_row, place)


def _attn_kernel(q_ref, k_ref, v_ref, fk_ref, o_ref, ka_ref, kb_ref, va_ref, vb_ref, s_ref, p_ref, m_ref,
                 acc_ref, *, t):
    i = pl.program_id(2)
    lane = lax.broadcasted_iota(jnp.int32, (1, LANES), 1)
    lo = lane < ATTN_HEAD_DIM
    den_lane = (ATTN_HEAD_DIM, 0)
    ones_at = lambda cond: jnp.where(cond, 1.0, 0.0).astype(BF16)

    @pl.when(i == 0)
    def _():
        k = k_ref[...]
        fk = fk_ref[...]
        ka_ref[...] = jnp.where(lo, k, fk)
        kb_ref[...] = jnp.where(lo, fk, k)
        v = v_ref[...]
        va_ref[...] = jnp.where(lo, v, ones_at(lane == den_lane[0]))
        vb_ref[...] = jnp.where(lo, ones_at(lane == den_lane[1]), v)

    q = q_ref[...]
    bias_a = ones_at((lane >= ATTN_HEAD_DIM) & (lane < ATTN_HEAD_DIM + N_FPARTS))
    bias_b = ones_at(lane < N_FPARTS)
    q_heads = (jnp.where(lo, q, bias_a), jnp.where(lo, bias_b, q))
    k_refs = (ka_ref, kb_ref)
    v_refs = (va_ref, vb_ref)
    m_ref[...] = jnp.full(m_ref.shape, NEG_BIG, F32)
    acc_ref[...] = jnp.zeros(acc_ref.shape, F32)

    def step(kb, masked):
        ks = pl.multiple_of(kb * t, t)
        for e in range(2):
            s = _dot_nt(q_heads[e], k_refs[e][pl.ds(ks, t), :])
            if masked:
                r = lax.broadcasted_iota(jnp.int32, (t, t), 0)
                c = lax.broadcasted_iota(jnp.int32, (t, t), 1)
                s = jnp.where(c <= r, s, NEG_BIG)
            s_ref[e] = s
        for e in range(2):
            m_prev = m_ref[e]
            m_new = jnp.maximum(m_prev, jnp.max(s_ref[e], axis=-1, keepdims=True))
            m_ref[e] = m_new
            p_ref[e] = jnp.exp2(s_ref[e] - jnp.concatenate([m_new] * (t // LANES), axis=1)).astype(BF16)
            pv = _dot(p_ref[e], v_refs[e][pl.ds(ks, t), :])
            acc_ref[e] = acc_ref[e] * jnp.exp2(m_prev - m_new) + pv

    def body(kb, carry):
        step(kb, False)
        return carry

    lax.fori_loop(0, i, body, 0)
    step(i, True)
    acc_a = acc_ref[0]
    acc_b = acc_ref[1]
    inv_a = 1.0 / acc_a[:, den_lane[0]:den_lane[0] + 1]
    inv_b = 1.0 / acc_b[:, den_lane[1]:den_lane[1] + 1]
    o_ref[...] = jnp.where(lo, acc_a * inv_a, acc_b * inv_b).astype(o_ref.dtype)


def _attention(qkv3, fk):
    B, S, _ = qkv3.shape
    t = min(ATTN_TILE, S)
    seq = lambda: pltpu.VMEM((S, LANES), BF16)
    return pl.pallas_call(
        functools.partial(_attn_kernel, t=t),
        grid=(B, HEAD_PAIRS, S // t),
        in_specs=[
            pl.BlockSpec((None, t, LANES), lambda b, h, i: (b, i, h)),
            pl.BlockSpec((None, S, LANES), lambda b, h, i: (b, 0, HEAD_PAIRS + h)),
            pl.BlockSpec((None, S, LANES), lambda b, h, i: (b, 0, 2 * HEAD_PAIRS + h)),
            pl.BlockSpec((None, None, S, LANES), lambda b, h, i: (b, h, 0, 0)),
        ],
        out_specs=pl.BlockSpec((None, t, LANES), lambda b, h, i: (b, i, h)),
        out_shape=jax.ShapeDtypeStruct((B, S, D_ATTN), BF16),
        scratch_shapes=[
            seq(), seq(), seq(), seq(),
            pltpu.VMEM((2, t, t), F32), pltpu.VMEM((2, t, t), BF16),
            pltpu.VMEM((2, t, LANES), F32), pltpu.VMEM((2, t, LANES), F32),
        ],
        compiler_params=_cparams(("parallel", "parallel", "arbitrary")),
        name="fox_attention",
    )(qkv3, qkv3, qkv3, fk)


def _ssd_kernel(xbc_ref, z_ref, small_ref, cw_ref, cb_ref, dtb_ref, alog_ref, dskip_ref, ng_ref,
                e1_ref, e2_ref, o_ref, xbuf_ref, state_ref, *, L):
    c = pl.program_id(1)
    halo = 8

    @pl.when(c == 0)
    def _():
        xbuf_ref[0:halo, :] = jnp.zeros((halo, CONV_CH), F32)
        state_ref[...] = jnp.zeros_like(state_ref)

    @pl.when(c > 0)
    def _():
        xbuf_ref[0:halo, :] = xbuf_ref[L:L + halo, :]

    xbuf_ref[halo:halo + L, :] = xbc_ref[...]
    w = cw_ref[...]
    conv = cb_ref[...] + w[3:4, :] * xbuf_ref[halo:halo + L, :]
    for kk in range(CONV_WIDTH - 1):
        conv = conv + w[kk:kk + 1, :] * xbuf_ref[pl.ds(halo - (CONV_WIDTH - 1) + kk, L), :]
    u = conv * _sigmoid(conv)
    xs = u[:, :D_SSD]
    bm = u[:, D_SSD:D_SSD + SSD_GROUPS * SSD_STATE]
    cm = u[:, D_SSD + SSD_GROUPS * SSD_STATE:]

    dt = _softplus(small_ref[...] + dtb_ref[...])
    a = dt * (-jnp.exp(alog_ref[...]))
    acs = _dot_01_x(_tril01(L), a)
    e1 = e1_ref[...]
    dt_e = _dot_x_01(dt, e1)
    acs_e = _dot_x_01(acs, e1)
    acs_b = _dot_x_01(acs, e2_ref[...])
    acs_t = acs.T
    last_e = acs_e[L - 1:L, :]
    x_dt = xs * dt_e
    x_dt_bf = x_dt.astype(BF16)
    x_end_bf = (x_dt * jnp.exp(last_e - acs_e)).astype(BF16)
    grow = jnp.exp(acs_e)
    chunk_decay = jnp.exp(last_e)

    row = lax.broadcasted_iota(jnp.int32, (L, L), 0)
    col = lax.broadcasted_iota(jnp.int32, (L, L), 1)
    causal = col <= row
    lo = lax.broadcasted_iota(jnp.int32, (1, LANES), 1) < SSD_HEAD_DIM
    pairs_per_group = SSD_HEADS // SSD_GROUPS // 2

    ys = []
    for g in range(SSD_GROUPS):
        bg = bm[:, g * SSD_STATE:(g + 1) * SSD_STATE]
        bg_bf = bg.astype(BF16)
        bg_t_bf = bg.T.astype(BF16)
        cg_bf = cm[:, g * SSD_STATE:(g + 1) * SSD_STATE].astype(BF16)
        cb = _dot_nt(cg_bf, bg_bf)
        for jj in range(pairs_per_group):
            j = g * pairs_per_group + jj
            sl = slice(j * LANES, (j + 1) * LANES)
            xp = x_dt_bf[:, sl]
            yd = []
            for e in range(2):
                h = 2 * j + e
                seg = acs_b[:, h * L:(h + 1) * L] - acs_t[HEAD_LANE0 + h:HEAD_LANE0 + h + 1, :]
                dec = jnp.exp(jnp.where(causal, seg, NEG_BIG))
                yd.append(_dot((cb * dec).astype(BF16), xp))
            st = state_ref[j]
            y_off = _dot(cg_bf, st.astype(BF16)) * grow[:, sl]
            ys.append(jnp.where(lo, yd[0], yd[1]) + y_off)
            state_ref[j] = st * chunk_decay[:, sl] + _dot(bg_t_bf, x_end_bf[:, sl])

    y = jnp.concatenate(ys, axis=1) + xs * dskip_ref[...]
    zz = z_ref[...]
    y = y * (zz * _sigmoid(zz))
    o_ref[...] = _rms(y, ng_ref[...]).astype(o_ref.dtype)


def _expand_mats(L):
    e1 = np.zeros((LANES, D_SSD), np.float32)
    e2 = np.zeros((LANES, SSD_HEADS * L), np.float32)
    for h in range(SSD_HEADS):
        e1[HEAD_LANE0 + h, h * SSD_HEAD_DIM:(h + 1) * SSD_HEAD_DIM] = 1.0
        e2[HEAD_LANE0 + h, h * L:(h + 1) * L] = 1.0
    return jnp.asarray(e1, BF16), jnp.asarray(e2, BF16)


def _ssd(xbc3, z3, small3, conv_w, conv_b, dtb_row, alog_row, dskip_row, norm_g):
    B, S, _ = xbc3.shape
    L = min(SSD_CHUNK, S)
    e1, e2 = _expand_mats(L)
    const = lambda shape: pl.BlockSpec(shape, lambda b, c: (0,) * len(shape))
    return pl.pallas_call(
        functools.partial(_ssd_kernel, L=L),
        grid=(B, S // L),
        in_specs=[
            pl.BlockSpec((None, L, CONV_CH), lambda b, c: (b, c, 0)),
            pl.BlockSpec((None, L, D_SSD), lambda b, c: (b, c, 0)),
            pl.BlockSpec((None, L, SMALL_W), lambda b, c: (b, c, 0)),
            const((CONV_WIDTH, CONV_CH)), const((1, CONV_CH)), const((1, SMALL_W)), const((1, SMALL_W)),
            const((1, D_SSD)), const((1, D_SSD)), const(e1.shape), const(e2.shape),
        ],
        out_specs=pl.BlockSpec((None, L, D_SSD), lambda b, c: (b, c, 0)),
        out_shape=jax.ShapeDtypeStruct((B, S, D_SSD), BF16),
        scratch_shapes=[pltpu.VMEM((L + 8, CONV_CH), F32),
                        pltpu.VMEM((SSD_HEADS // 2, SSD_STATE, LANES), F32)],
        compiler_params=_cparams(("parallel", "arbitrary")),
        name="conv_ssd",
    )(xbc3, z3, small3, conv_w, conv_b, dtb_row, alog_row, dskip_row, norm_g, e1, e2)


def _merge_kernel(ya_ref, ys_ref, gates_ref, x_ref, wa_ref, ws_ref, wo_ref, g2_ref, wr_ref, br_ref,
                  xo_ref, h_ref, lg_ref):
    D = x_ref.shape[-1]
    pa = _dot(ya_ref[...], wa_ref[...])
    ps = _dot(ys_ref[...], ws_ref[...])
    merged = _sigmoid(gates_ref[:, :D]) * pa + _sigmoid(gates_ref[:, D:]) * ps
    xn = x_ref[...] + _dot(merged.astype(BF16), wo_ref[...])
    xo_ref[...] = xn
    h = _rms(xn, g2_ref[...])
    h_ref[...] = h.astype(BF16)
    h_hi, h_mid, _ = _split3(h)
    w_hi, w_mid, _ = _split3(wr_ref[...])
    lg_ref[...] = _dot(h_hi, w_hi) + _dot(h_hi, w_mid) + _dot(h_mid, w_hi) + br_ref[...]


def _merge(ya, ys, gates, x2, wa, ws, wo, g2, wr, br):
    T, D = x2.shape
    tm = min(ROW_TILE, T)
    row = lambda n: pl.BlockSpec((tm, n), lambda i: (i, 0))
    const = lambda a: pl.BlockSpec(a.shape, lambda i: (0, 0))
    return pl.pallas_call(
        _merge_kernel,
        grid=(T // tm,),
        in_specs=[row(D_ATTN), row(D_SSD), row(2 * D), row(D),
                  const(wa), const(ws), const(wo), const(g2), const(wr), const(br)],
        out_specs=[row(D), row(D), row(LANES)],
        out_shape=[jax.ShapeDtypeStruct((T, D), F32), jax.ShapeDtypeStruct((T, D), BF16),
                   jax.ShapeDtypeStruct((T, LANES), F32)],
        compiler_params=_cparams(("parallel",)),
        name="merge_outproj_router",
    )(ya, ys, gates, x2, wa, ws, wo, g2, wr, br)


def _route_kernel(lg_ref, route_ref, cnt_ref, carry_ref):
    i = pl.program_id(0)

    @pl.when(i == 0)
    def _():
        carry_ref[...] = jnp.zeros_like(carry_ref)

    lg = lg_ref[...]
    tm = lg.shape[0]
    lane_i = lax.broadcasted_iota(jnp.int32, (tm, LANES), 1)
    lane = lane_i.astype(F32)
    first = lambda cond: jnp.min(jnp.where(cond, lane, float(LANES)), axis=-1, keepdims=True)

    gmask = lane_i < N_EXPERT_GROUPS
    gl = jnp.where(gmask, lg, NEG_BIG)
    gmax = jnp.max(gl, axis=-1, keepdims=True)
    gsum = jnp.sum(jnp.where(gmask, jnp.exp(gl - gmax), 0.0), axis=-1, keepdims=True)
    g_idx = first(gmask & (gl == gmax))
    g_w = 1.0 / gsum

    e_lo = N_EXPERT_GROUPS + g_idx * EXPERTS_PER_GROUP
    emask = (lane >= e_lo) & (lane < e_lo + EXPERTS_PER_GROUP)
    el = jnp.where(emask, lg, NEG_BIG)
    emax = jnp.max(el, axis=-1, keepdims=True)
    esum = jnp.sum(jnp.where(emask, jnp.exp(el - emax), 0.0), axis=-1, keepdims=True)
    l0 = first(emask & (el == emax))
    el2 = jnp.where(lane == l0, NEG_BIG, el)
    m2 = jnp.max(el2, axis=-1, keepdims=True)
    l1 = first(emask & (lane != l0) & (el2 == m2))
    gate0 = g_w * (1.0 / esum)
    gate1 = g_w * (jnp.exp(m2 - emax) / esum)

    is0 = lane == l0
    is1 = lane == l1
    onehot = jnp.where(is0 | is1, 1.0, 0.0)
    before = _dot(_tril01(tm, strict=True), onehot.astype(BF16)) + carry_ref[0:1, :]
    rank0 = jnp.sum(jnp.where(is0, before, 0.0), axis=-1, keepdims=True)
    rank1 = jnp.sum(jnp.where(is1, before, 0.0), axis=-1, keepdims=True)
    total = carry_ref[0:1, :] + jnp.sum(onehot, axis=0, keepdims=True)
    carry_ref[0:1, :] = total
    cnt_ref[...] = jnp.broadcast_to(total, cnt_ref.shape)

    rec = jnp.zeros((tm, LANES), F32)
    for ln, val in ((R_E0, l0 - N_EXPERT_GROUPS), (R_E1, l1 - N_EXPERT_GROUPS), (R_G0, gate0), (R_G1, gate1),
                    (R_RANK0, rank0), (R_RANK1, rank1)):
        rec = jnp.where(lane_i == ln, val, rec)
    route_ref[...] = rec


def _route(logits):
    T = logits.shape[0]
    tm = min(ROW_TILE, T)
    return pl.pallas_call(
        _route_kernel,
        grid=(T // tm,),
        in_specs=[pl.BlockSpec((tm, LANES), lambda i: (i, 0))],
        out_specs=[pl.BlockSpec((tm, LANES), lambda i: (i, 0)), pl.BlockSpec((8, LANES), lambda i: (0, 0))],
        out_shape=[jax.ShapeDtypeStruct((T, LANES), F32), jax.ShapeDtypeStruct((8, LANES), F32)],
        scratch_shapes=[pltpu.VMEM((8, LANES), F32)],
        compiler_params=_cparams(("arbitrary",)),
        name="route",
    )(logits)


def _table_lookup(idx, table):
    n = table.shape[0]
    return jnp.sum(jnp.where(idx[:, None] == jnp.arange(n, dtype=idx.dtype)[None, :], table[None, :], 0), axis=1)


def _dispatch_plan(route, cnt):
    T = route.shape[0]
    counts = cnt[0, N_EXPERT_GROUPS:N_EXPERT_GROUPS + N_EXPERTS].astype(jnp.int32)
    padded = (counts + MOE_BLOCK - 1) // MOE_BLOCK * MOE_BLOCK
    pad_end = jnp.cumsum(padded)
    pad_start = pad_end - padded
    start = jnp.cumsum(counts) - counts
    e0 = route[:, R_E0].astype(jnp.int32)
    e1 = route[:, R_E1].astype(jnp.int32)
    dest0 = _table_lookup(e0, pad_start) + route[:, R_RANK0].astype(jnp.int32)
    dest1 = _table_lookup(e1, pad_start) + route[:, R_RANK1].astype(jnp.int32)
    n_blocks = -(-(T * TOP_K) // MOE_BLOCK) + N_EXPERTS
    block_end = pad_end // MOE_BLOCK
    blk = jnp.arange(n_blocks, dtype=jnp.int32)
    block_expert = jnp.minimum(jnp.sum(blk[:, None] >= block_end[None, :], axis=1), N_EXPERTS - 1).astype(jnp.int32)
    n_used = block_end[-1:].astype(jnp.int32)
    order = jnp.argsort(jnp.stack([e0, e1], axis=1).reshape(-1), stable=True).astype(jnp.int32)
    shift = _table_lookup(block_expert, start - pad_start)
    pos = jnp.arange(n_blocks * MOE_BLOCK, dtype=jnp.int32) + jnp.repeat(shift, MOE_BLOCK)
    src_tok = order[jnp.clip(pos, 0, T * TOP_K - 1)] // TOP_K
    return dest0, dest1, src_tok, block_expert, n_used


def _expert_kernel(be_ref, nu_ref, x_ref, w1_ref, w3_ref, w2_ref, o_ref, w1b_ref, w3b_ref, w2b_ref):
    i = pl.program_id(0)
    last = nu_ref[0] - 1
    cur = be_ref[jnp.minimum(i, last)]
    prev = be_ref[jnp.minimum(jnp.maximum(i - 1, 0), last)]

    @pl.when((i == 0) | (cur != prev))
    def _():
        w1b_ref[...] = w1_ref[...].astype(BF16)
        w3b_ref[...] = w3_ref[...].astype(BF16)
        w2b_ref[...] = w2_ref[...].astype(BF16)

    @pl.when(i <= last)
    def _():
        x = x_ref[...]
        h1 = _dot(x, w1b_ref[...])
        h3 = _dot(x, w3b_ref[...])
        a = (h1 * _sigmoid(h1) * h3).astype(BF16)
        o_ref[...] = _dot(a, w2b_ref[...])


def _experts(buf, block_expert, n_used, w1, w3, w2):
    R, D = buf.shape
    FF = w1.shape[-1]
    nb = R // MOE_BLOCK
    blk = lambda i, be, nu: (jnp.minimum(i, nu[0] - 1), 0)
    wsel = lambda i, be, nu: (be[jnp.minimum(i, nu[0] - 1)], 0, 0)
    return pl.pallas_call(
        _expert_kernel,
        grid_spec=pltpu.PrefetchScalarGridSpec(
            num_scalar_prefetch=2,
            grid=(nb,),
            in_specs=[
                pl.BlockSpec((MOE_BLOCK, D), blk),
                pl.BlockSpec((None, D, FF), wsel),
                pl.BlockSpec((None, D, FF), wsel),
                pl.BlockSpec((None, FF, D), wsel),
            ],
            out_specs=pl.BlockSpec((MOE_BLOCK, D), blk),
            scratch_shapes=[pltpu.VMEM((D, FF), BF16), pltpu.VMEM((D, FF), BF16), pltpu.VMEM((FF, D), BF16)],
        ),
        out_shape=jax.ShapeDtypeStruct((R, D), F32),
        compiler_params=_cparams(("arbitrary",)),
        name="expert_mlp",
    )(block_expert, n_used, buf, w1, w3, w2)


def _final_kernel(x_ref, r0_ref, r1_ref, rt_ref, g_ref, o_ref):
    o_ref[...] = _rms(_moe_combine(x_ref, r0_ref, r1_ref, rt_ref), g_ref[...])


def _final_norm(x2, moe, g):
    T, D = x2.shape
    tm = min(ROW_TILE, T)
    row = lambda n: pl.BlockSpec((tm, n), lambda i: (i, 0))
    return pl.pallas_call(
        _final_kernel,
        grid=(T // tm,),
        in_specs=[row(D), row(D), row(D), row(LANES), pl.BlockSpec((1, D), lambda i: (0, 0))],
        out_specs=row(D),
        out_shape=jax.ShapeDtypeStruct((T, D), F32),
        compiler_params=_cparams(("parallel",)),
        name="final_norm",
    )(x2, *moe, g)


def _pack_w_in(w_in):
    depth, D, _ = w_in.shape
    widths = (D_ATTN, D_ATTN, D_ATTN, ATTN_HEADS, D_SSD, CONV_CH, SSD_HEADS, D, D)
    offs = np.concatenate([[0], np.cumsum(widths)])
    q, k, v, f, z, xbc, dt, ga, gs = (w_in[:, :, offs[i]:offs[i + 1]] for i in range(len(widths)))
    pad = jnp.zeros((depth, D, SMALL_W - ATTN_HEADS - SSD_HEADS), w_in.dtype)
    scale = ATTN_HEAD_DIM ** -0.5 * LOG2E
    return jnp.concatenate([q * scale, k, v, z, xbc, ga, gs, f, dt, pad], axis=2).astype(BF16)


def _small_rows(mat, lane0):
    depth, n = mat.shape
    return jnp.zeros((depth, 1, SMALL_W), F32).at[:, 0, lane0:lane0 + n].set(mat.astype(F32))


def kernel(x, norm_mix_g, w_in, b_f, conv_w, conv_b, dt_bias, a_log, d_skip, ssd_norm_g, w_br_attn, w_br_ssd,
           w_out, norm_ffn_g, w_group_router, b_group_router, w_expert_router, b_expert_router, w1, w3, w2,
           final_g):
    B, S, D = x.shape
    T = B * S
    depth = w_in.shape[0]
    w_packed = _pack_w_in(w_in)
    bf_rows = _small_rows(b_f, 0)
    dtb_rows = _small_rows(dt_bias, HEAD_LANE0)
    alog_rows = _small_rows(a_log, HEAD_LANE0)
    dskip_rows = jnp.repeat(d_skip, SSD_HEAD_DIM, axis=1)[:, None, :]
    router_pad = LANES - N_EXPERT_GROUPS - N_EXPERTS
    w_router = jnp.concatenate([w_group_router, w_expert_router, jnp.zeros((depth, D, router_pad), F32)], axis=2)
    b_router = jnp.concatenate([b_group_router, b_expert_router, jnp.zeros((depth, router_pad), F32)],
                               axis=1)[:, None, :]
    wa, ws, wo = w_br_attn.astype(BF16), w_br_ssd.astype(BF16), w_out.astype(BF16)

    x2 = x.reshape(T, D)
    moe = None
    for l in range(depth):
        x2, (qkv, z, xbc, gates, small) = _inproj(x2, moe, norm_mix_g[l][None, :], w_packed[l])
        small3 = small.reshape(B, S, SMALL_W)
        fk = _fcum(small3, bf_rows[l])
        y_attn = _attention(qkv.reshape(B, S, 3 * D_ATTN), fk).reshape(T, D_ATTN)
        y_ssd = _ssd(xbc.reshape(B, S, CONV_CH), z.reshape(B, S, D_SSD), small3, conv_w[l], conv_b[l][None, :],
                     dtb_rows[l], alog_rows[l], dskip_rows[l], ssd_norm_g[l][None, :]).reshape(T, D_SSD)
        x2, h2, logits = _merge(y_attn, y_ssd, gates, x2, wa[l], ws[l], wo[l], norm_ffn_g[l][None, :],
                                w_router[l], b_router[l])
        route, cnt = _route(logits)
        dest0, dest1, src_tok, block_expert, n_used = _dispatch_plan(route, cnt)
        out = _experts(h2[src_tok], block_expert, n_used, w1[l], w3[l], w2[l])
        moe = (out[dest0], out[dest1], route)
    return _final_norm(x2, moe, final_g[None, :]).reshape(B, S, D)
```

```python
import functools

import numpy as np
import jax
import jax.numpy as jnp
from jax import lax
from jax.experimental import pallas as pl
from jax.experimental.pallas import tpu as pltpu

F32 = jnp.float32
BF16 = jnp.bfloat16

ATTN_HEADS = 8
ATTN_HEAD_DIM = 64
D_ATTN = ATTN_HEADS * ATTN_HEAD_DIM
SSD_HEADS = 16
SSD_HEAD_DIM = 64
D_SSD = SSD_HEADS * SSD_HEAD_DIM
SSD_GROUPS = 2
SSD_STATE = 128
CONV_WIDTH = 4
CONV_CH = D_SSD + 2 * SSD_GROUPS * SSD_STATE
N_EXPERT_GROUPS = 4
EXPERTS_PER_GROUP = 8
N_EXPERTS = N_EXPERT_GROUPS * EXPERTS_PER_GROUP
TOP_K = 2
EPS = 1e-6

LANES = 128
HEAD_LANE0 = 8
SMALL_W = LANES
NEG_BIG = -1e30
LOG2E = 1.4426950408889634
VMEM_LIMIT = 56 * 1024 * 1024

SSD_CHUNK = 128
ATTN_TILE = 512
FCUM_TILE = 512
ROW_TILE = 512
MOE_BLOCK = 256
BATCH_SPLIT = 2
R_E0, R_E1, R_G0, R_G1, R_RANK0, R_RANK1 = range(6)


def _cparams(sem):
    return pltpu.CompilerParams(dimension_semantics=sem, vmem_limit_bytes=VMEM_LIMIT)


def _split3(x):
    hi = x.astype(BF16)
    r = x - hi.astype(F32)
    mid = r.astype(BF16)
    lo = (r - mid.astype(F32)).astype(BF16)
    return hi, mid, lo


def _dot(a, b):
    return jnp.dot(a, b, preferred_element_type=F32)


def _dot_nt(a, b):
    return lax.dot_general(a, b, (((1,), (1,)), ((), ())), preferred_element_type=F32)


def _dot_x_01(x, m01):
    hi, mid, lo = _split3(x)
    return _dot(hi, m01) + _dot(mid, m01) + _dot(lo, m01)


def _dot_01_x(m01, x):
    hi, mid, lo = _split3(x)
    return _dot(m01, hi) + _dot(m01, mid) + _dot(m01, lo)


def _tril01(n, strict=False):
    r = lax.broadcasted_iota(jnp.int32, (n, n), 0)
    c = lax.broadcasted_iota(jnp.int32, (n, n), 1)
    return ((c < r) if strict else (c <= r)).astype(BF16)


def _softplus(x):
    return jnp.maximum(x, 0.0) + jnp.log1p(jnp.exp(-jnp.abs(x)))


def _sigmoid(x):
    return 0.5 * jnp.tanh(0.5 * x) + 0.5


def _rms(x, g):
    ms = jnp.mean(x * x, axis=-1, keepdims=True)
    return x * lax.rsqrt(ms + EPS) * g


def _moe_combine(x_ref, r0_ref, r1_ref, rt_ref):
    rt = rt_ref[...]
    return x_ref[...] + (rt[:, R_G0:R_G0 + 1] * r0_ref[...] + rt[:, R_G1:R_G1 + 1] * r1_ref[...])


def _inproj_kernel(*refs, with_moe):
    if with_moe:
        x_ref, r0_ref, r1_ref, rt_ref, g_ref, w_ref, xo_ref, *outs = refs
        x = _moe_combine(x_ref, r0_ref, r1_ref, rt_ref)
        xo_ref[...] = x
    else:
        x_ref, g_ref, w_ref, *outs = refs
        x = x_ref[...]
    h = _rms(x, g_ref[...]).astype(BF16)
    off = 0
    for ref in outs:
        n = ref.shape[-1]
        for c0 in range(0, n, 512):
            c1 = min(c0 + 512, n)
            ref[:, c0:c1] = _dot(h, w_ref[:, off + c0:off + c1]).astype(ref.dtype)
        off += n


def _inproj(x2, moe, g, w_packed):
    T, D = x2.shape
    widths = (3 * D_ATTN, D_SSD, CONV_CH, 2 * D, SMALL_W)
    dtypes = (BF16, F32, F32, F32, F32)
    tm = min(ROW_TILE, T)
    row = lambda n: pl.BlockSpec((tm, n), lambda i: (i, 0))
    with_moe = moe is not None
    ins = [x2] + (list(moe) if with_moe else []) + [g, w_packed]
    in_specs = [row(D)] + ([row(D), row(D), row(LANES)] if with_moe else []) + [
        pl.BlockSpec((1, D), lambda i: (0, 0)),
        pl.BlockSpec(w_packed.shape, lambda i: (0, 0), pipeline_mode=pl.Buffered(1)),
    ]
    out_specs = ([row(D)] if with_moe else []) + [row(n) for n in widths]
    out_shape = ([jax.ShapeDtypeStruct((T, D), F32)] if with_moe else []) + [
        jax.ShapeDtypeStruct((T, n), dt) for n, dt in zip(widths, dtypes)]
    res = pl.pallas_call(
        functools.partial(_inproj_kernel, with_moe=with_moe),
        grid=(T // tm,),
        in_specs=in_specs,
        out_specs=out_specs,
        out_shape=out_shape,
        compiler_params=_cparams(("parallel",)),
        name="inproj",
    )(*ins)
    return (res[0], res[1:]) if with_moe else (x2, res)


N_FPARTS = 3
HEAD_PAIRS = D_ATTN // LANES


def _fcum_kernel(s_ref, b_ref, place_ref, fk_ref, carry_ref):
    c = pl.program_id(1)

    @pl.when(c == 0)
    def _():
        carry_ref[...] = jnp.zeros_like(carry_ref)

    x = s_ref[...] + b_ref[...]
    logf = jnp.minimum(x, 0.0) - jnp.log1p(jnp.exp(-jnp.abs(x)))
    n = x.shape[0]
    cum = _dot_01_x(_tril01(n), logf) + carry_ref[0:1, :]
    carry_ref[0:1, :] = cum[n - 1:n, :]
    parts = jnp.concatenate(_split3(cum * (-LOG2E)), axis=1)
    for p in range(HEAD_PAIRS):
        fk_ref[p] = _dot(parts, place_ref[p]).astype(BF16)


def _fpart_placement():
    m = np.zeros((HEAD_PAIRS, N_FPARTS * LANES, LANES), np.float32)
    for p in range(HEAD_PAIRS):
        for j in range(N_FPARTS):
            m[p, j * LANES + 2 * p, ATTN_HEAD_DIM + j] = 1.0
            m[p, j * LANES + 2 * p + 1, j] = 1.0
    return jnp.asarray(m, BF16)


def _fcum(small3, bf_row):
    B, S, _ = small3.shape
    L = min(FCUM_TILE, S)
    place = _fpart_placement()
    return pl.pallas_call(
        _fcum_kernel,
        grid=(B, S // L),
        in_specs=[
            pl.BlockSpec((None, L, SMALL_W), lambda b, c: (b, c, 0)),
            pl.BlockSpec((1, SMALL_W), lambda b, c: (0, 0)),
            pl.BlockSpec(place.shape, lambda b, c: (0, 0, 0)),
        ],
        out_specs=pl.BlockSpec((None, HEAD_PAIRS, L, LANES), lambda b, c: (b, 0, c, 0)),
        out_shape=jax.ShapeDtypeStruct((B, HEAD_PAIRS, S, LANES), BF16),
        scratch_shapes=[pltpu.VMEM((8, SMALL_W), F32)],
        compiler_params=_cparams(("parallel", "arbitrary")),
        name="forget_cumsum",
    )(small3, bf_row, place)


def _attn_kernel(q_ref, k_ref, v_ref, fk_ref, o_ref, ka_ref, kb_ref, va_ref, vb_ref, s_ref, p_ref, m_ref,
                 acc_ref, *, t):
    i = pl.program_id(2)
    lane = lax.broadcasted_iota(jnp.int32, (1, LANES), 1)
    lo = lane < ATTN_HEAD_DIM
    den_lane = (ATTN_HEAD_DIM, 0)
    ones_at = lambda cond: jnp.where(cond, 1.0, 0.0).astype(BF16)

    @pl.when(i == 0)
    def _():
        k = k_ref[...]
        fk = fk_ref[...]
        ka_ref[...] = jnp.where(lo, k, fk)
        kb_ref[...] = jnp.where(lo, fk, k)
        v = v_ref[...]
        va_ref[...] = jnp.where(lo, v, ones_at(lane == den_lane[0]))
        vb_ref[...] = jnp.where(lo, ones_at(lane == den_lane[1]), v)

    q = q_ref[...]
    bias_a = ones_at((lane >= ATTN_HEAD_DIM) & (lane < ATTN_HEAD_DIM + N_FPARTS))
    bias_b = ones_at(lane < N_FPARTS)
    q_heads = (jnp.where(lo, q, bias_a), jnp.where(lo, bias_b, q))
    k_refs = (ka_ref, kb_ref)
    v_refs = (va_ref, vb_ref)
    m_ref[...] = jnp.full(m_ref.shape, NEG_BIG, F32)
    acc_ref[...] = jnp.zeros(acc_ref.shape, F32)

    def step(kb, masked):
        ks = pl.multiple_of(kb * t, t)
        for e in range(2):
            s = _dot_nt(q_heads[e], k_refs[e][pl.ds(ks, t), :])
            if masked:
                r = lax.broadcasted_iota(jnp.int32, (t, t), 0)
                c = lax.broadcasted_iota(jnp.int32, (t, t), 1)
                s = jnp.where(c <= r, s, NEG_BIG)
            s_ref[e] = s
        for e in range(2):
            m_prev = m_ref[e]
            m_new = jnp.maximum(m_prev, jnp.max(s_ref[e], axis=-1, keepdims=True))
            m_ref[e] = m_new
            p_ref[e] = jnp.exp2(s_ref[e] - jnp.concatenate([m_new] * (t // LANES), axis=1)).astype(BF16)
            pv = _dot(p_ref[e], v_refs[e][pl.ds(ks, t), :])
            acc_ref[e] = acc_ref[e] * jnp.exp2(m_prev - m_new) + pv

    def body(kb, carry):
        step(kb, False)
        return carry

    lax.fori_loop(0, i, body, 0)
    step(i, True)
    acc_a = acc_ref[0]
    acc_b = acc_ref[1]
    inv_a = 1.0 / acc_a[:, den_lane[0]:den_lane[0] + 1]
    inv_b = 1.0 / acc_b[:, den_lane[1]:den_lane[1] + 1]
    o_ref[...] = jnp.where(lo, acc_a * inv_a, acc_b * inv_b).astype(o_ref.dtype)


def _attention(qkv3, fk):
    B, S, _ = qkv3.shape
    t = min(ATTN_TILE, S)
    seq = lambda: pltpu.VMEM((S, LANES), BF16)
    return pl.pallas_call(
        functools.partial(_attn_kernel, t=t),
        grid=(B, HEAD_PAIRS, S // t),
        in_specs=[
            pl.BlockSpec((None, t, LANES), lambda b, h, i: (b, i, h)),
            pl.BlockSpec((None, S, LANES), lambda b, h, i: (b, 0, HEAD_PAIRS + h)),
            pl.BlockSpec((None, S, LANES), lambda b, h, i: (b, 0, 2 * HEAD_PAIRS + h)),
            pl.BlockSpec((None, None, S, LANES), lambda b, h, i: (b, h, 0, 0)),
        ],
        out_specs=pl.BlockSpec((None, t, LANES), lambda b, h, i: (b, i, h)),
        out_shape=jax.ShapeDtypeStruct((B, S, D_ATTN), BF16),
        scratch_shapes=[
            seq(), seq(), seq(), seq(),
            pltpu.VMEM((2, t, t), F32), pltpu.VMEM((2, t, t), BF16),
            pltpu.VMEM((2, t, LANES), F32), pltpu.VMEM((2, t, LANES), F32),
        ],
        compiler_params=_cparams(("parallel", "parallel", "arbitrary")),
        name="fox_attention",
    )(qkv3, qkv3, qkv3, fk)


def _ssd_kernel(xbc_ref, z_ref, small_ref, cw_ref, cb_ref, dtb_ref, alog_ref, dskip_ref, ng_ref,
                e1_ref, o_ref, xbuf_ref, state_ref, *, L):
    c = pl.program_id(1)
    halo = 8

    @pl.when(c == 0)
    def _():
        xbuf_ref[0:halo, :] = jnp.zeros((halo, CONV_CH), F32)
        state_ref[...] = jnp.zeros_like(state_ref)

    @pl.when(c > 0)
    def _():
        xbuf_ref[0:halo, :] = xbuf_ref[L:L + halo, :]

    xbuf_ref[halo:halo + L, :] = xbc_ref[...]
    w = cw_ref[...]
    conv = cb_ref[...] + w[3:4, :] * xbuf_ref[halo:halo + L, :]
    for kk in range(CONV_WIDTH - 1):
        conv = conv + w[kk:kk + 1, :] * xbuf_ref[pl.ds(halo - (CONV_WIDTH - 1) + kk, L), :]
    u = conv * _sigmoid(conv)
    xs = u[:, :D_SSD]
    bm = u[:, D_SSD:D_SSD + SSD_GROUPS * SSD_STATE]
    cm = u[:, D_SSD + SSD_GROUPS * SSD_STATE:]

    dt = _softplus(small_ref[...] + dtb_ref[...])
    a = dt * (-jnp.exp(alog_ref[...]))
    acs = _dot_01_x(_tril01(L), a)
    e1 = e1_ref[...]
    dt_e = _dot_x_01(dt, e1)
    acs_e = _dot_x_01(acs, e1)
    acs_t = acs.T
    last_e = acs_e[L - 1:L, :]
    x_dt = xs * dt_e
    x_dt_bf = x_dt.astype(BF16)
    x_end_bf = (x_dt * jnp.exp(last_e - acs_e)).astype(BF16)
    grow = jnp.exp(acs_e)
    chunk_decay = jnp.exp(last_e)

    row = lax.broadcasted_iota(jnp.int32, (L, L), 0)
    col = lax.broadcasted_iota(jnp.int32, (L, L), 1)
    causal = col <= row
    lo = lax.broadcasted_iota(jnp.int32, (1, LANES), 1) < SSD_HEAD_DIM
    pairs_per_group = SSD_HEADS // SSD_GROUPS // 2

    ys = []
    for g in range(SSD_GROUPS):
        bg = bm[:, g * SSD_STATE:(g + 1) * SSD_STATE]
        bg_bf = bg.astype(BF16)
        bg_t_bf = bg.T.astype(BF16)
        cg_bf = cm[:, g * SSD_STATE:(g + 1) * SSD_STATE].astype(BF16)
        cb = _dot_nt(cg_bf, bg_bf)
        for jj in range(pairs_per_group):
            j = g * pairs_per_group + jj
            sl = slice(j * LANES, (j + 1) * LANES)
            xp = x_dt_bf[:, sl]
            yd = []
            for e in range(2):
                h = 2 * j + e
                hl = HEAD_LANE0 + h
                seg = jnp.broadcast_to(acs[:, hl:hl + 1], (L, L)) - acs_t[hl:hl + 1, :]
                dec = jnp.exp(jnp.where(causal, seg, NEG_BIG))
                yd.append(_dot((cb * dec).astype(BF16), xp))
            st = state_ref[j]
            y_off = _dot(cg_bf, st.astype(BF16)) * grow[:, sl]
            ys.append(jnp.where(lo, yd[0], yd[1]) + y_off)
            state_ref[j] = st * chunk_decay[:, sl] + _dot(bg_t_bf, x_end_bf[:, sl])

    y = jnp.concatenate(ys, axis=1) + xs * dskip_ref[...]
    zz = z_ref[...]
    y = y * (zz * _sigmoid(zz))
    o_ref[...] = _rms(y, ng_ref[...]).astype(o_ref.dtype)


def _head_expand_mat():
    e1 = np.zeros((LANES, D_SSD), np.float32)
    for h in range(SSD_HEADS):
        e1[HEAD_LANE0 + h, h * SSD_HEAD_DIM:(h + 1) * SSD_HEAD_DIM] = 1.0
    return jnp.asarray(e1, BF16)


def _ssd(xbc3, z3, small3, conv_w, conv_b, dtb_row, alog_row, dskip_row, norm_g):
    B, S, _ = xbc3.shape
    L = min(SSD_CHUNK, S)
    e1 = _head_expand_mat()
    const = lambda shape: pl.BlockSpec(shape, lambda b, c: (0,) * len(shape))
    return pl.pallas_call(
        functools.partial(_ssd_kernel, L=L),
        grid=(B, S // L),
        in_specs=[
            pl.BlockSpec((None, L, CONV_CH), lambda b, c: (b, c, 0)),
            pl.BlockSpec((None, L, D_SSD), lambda b, c: (b, c, 0)),
            pl.BlockSpec((None, L, SMALL_W), lambda b, c: (b, c, 0)),
            const((CONV_WIDTH, CONV_CH)), const((1, CONV_CH)), const((1, SMALL_W)), const((1, SMALL_W)),
            const((1, D_SSD)), const((1, D_SSD)), const(e1.shape),
        ],
        out_specs=pl.BlockSpec((None, L, D_SSD), lambda b, c: (b, c, 0)),
        out_shape=jax.ShapeDtypeStruct((B, S, D_SSD), BF16),
        scratch_shapes=[pltpu.VMEM((L + 8, CONV_CH), F32),
                        pltpu.VMEM((SSD_HEADS // 2, SSD_STATE, LANES), F32)],
        compiler_params=_cparams(("parallel", "arbitrary")),
        name="conv_ssd",
    )(xbc3, z3, small3, conv_w, conv_b, dtb_row, alog_row, dskip_row, norm_g, e1)


def _merge_kernel(ya_ref, ys_ref, gates_ref, x_ref, wa_ref, ws_ref, wo_ref, g2_ref, wr_ref, br_ref,
                  xo_ref, h_ref, lg_ref):
    D = x_ref.shape[-1]
    pa = _dot(ya_ref[...], wa_ref[...])
    ps = _dot(ys_ref[...], ws_ref[...])
    merged = _sigmoid(gates_ref[:, :D]) * pa + _sigmoid(gates_ref[:, D:]) * ps
    xn = x_ref[...] + _dot(merged.astype(BF16), wo_ref[...])
    xo_ref[...] = xn
    h = _rms(xn, g2_ref[...])
    h_ref[...] = h.astype(BF16)
    h_hi, h_mid, _ = _split3(h)
    w_hi, w_mid, _ = _split3(wr_ref[...])
    lg_ref[...] = _dot(h_hi, w_hi) + _dot(h_hi, w_mid) + _dot(h_mid, w_hi) + br_ref[...]


def _merge(ya, ys, gates, x2, wa, ws, wo, g2, wr, br):
    T, D = x2.shape
    tm = min(ROW_TILE, T)
    row = lambda n: pl.BlockSpec((tm, n), lambda i: (i, 0))
    const = lambda a: pl.BlockSpec(a.shape, lambda i: (0, 0))
    return pl.pallas_call(
        _merge_kernel,
        grid=(T // tm,),
        in_specs=[row(D_ATTN), row(D_SSD), row(2 * D), row(D),
                  const(wa), const(ws), const(wo), const(g2), const(wr), const(br)],
        out_specs=[row(D), row(D), row(LANES)],
        out_shape=[jax.ShapeDtypeStruct((T, D), F32), jax.ShapeDtypeStruct((T, D), BF16),
                   jax.ShapeDtypeStruct((T, LANES), F32)],
        compiler_params=_cparams(("parallel",)),
        name="merge_outproj_router",
    )(ya, ys, gates, x2, wa, ws, wo, g2, wr, br)


def _route_kernel(lg_ref, route_ref, cnt_ref, carry_ref):
    i = pl.program_id(0)

    @pl.when(i == 0)
    def _():
        carry_ref[...] = jnp.zeros_like(carry_ref)

    lg = lg_ref[...]
    tm = lg.shape[0]
    lane_i = lax.broadcasted_iota(jnp.int32, (tm, LANES), 1)
    lane = lane_i.astype(F32)
    first = lambda cond: jnp.min(jnp.where(cond, lane, float(LANES)), axis=-1, keepdims=True)

    gmask = lane_i < N_EXPERT_GROUPS
    gl = jnp.where(gmask, lg, NEG_BIG)
    gmax = jnp.max(gl, axis=-1, keepdims=True)
    gsum = jnp.sum(jnp.where(gmask, jnp.exp(gl - gmax), 0.0), axis=-1, keepdims=True)
    g_idx = first(gmask & (gl == gmax))
    g_w = 1.0 / gsum

    e_lo = N_EXPERT_GROUPS + g_idx * EXPERTS_PER_GROUP
    emask = (lane >= e_lo) & (lane < e_lo + EXPERTS_PER_GROUP)
    el = jnp.where(emask, lg, NEG_BIG)
    emax = jnp.max(el, axis=-1, keepdims=True)
    esum = jnp.sum(jnp.where(emask, jnp.exp(el - emax), 0.0), axis=-1, keepdims=True)
    l0 = first(emask & (el == emax))
    el2 = jnp.where(lane == l0, NEG_BIG, el)
    m2 = jnp.max(el2, axis=-1, keepdims=True)
    l1 = first(emask & (lane != l0) & (el2 == m2))
    gate0 = g_w * (1.0 / esum)
    gate1 = g_w * (jnp.exp(m2 - emax) / esum)

    is0 = lane == l0
    is1 = lane == l1
    onehot = jnp.where(is0 | is1, 1.0, 0.0)
    before = _dot(_tril01(tm, strict=True), onehot.astype(BF16)) + carry_ref[0:1, :]
    rank0 = jnp.sum(jnp.where(is0, before, 0.0), axis=-1, keepdims=True)
    rank1 = jnp.sum(jnp.where(is1, before, 0.0), axis=-1, keepdims=True)
    total = carry_ref[0:1, :] + jnp.sum(onehot, axis=0, keepdims=True)
    carry_ref[0:1, :] = total
    cnt_ref[...] = jnp.broadcast_to(total, cnt_ref.shape)

    rec = jnp.zeros((tm, LANES), F32)
    for ln, val in ((R_E0, l0 - N_EXPERT_GROUPS), (R_E1, l1 - N_EXPERT_GROUPS), (R_G0, gate0), (R_G1, gate1),
                    (R_RANK0, rank0), (R_RANK1, rank1)):
        rec = jnp.where(lane_i == ln, val, rec)
    route_ref[...] = rec


def _route(logits):
    T = logits.shape[0]
    tm = min(ROW_TILE, T)
    return pl.pallas_call(
        _route_kernel,
        grid=(T // tm,),
        in_specs=[pl.BlockSpec((tm, LANES), lambda i: (i, 0))],
        out_specs=[pl.BlockSpec((tm, LANES), lambda i: (i, 0)), pl.BlockSpec((8, LANES), lambda i: (0, 0))],
        out_shape=[jax.ShapeDtypeStruct((T, LANES), F32), jax.ShapeDtypeStruct((8, LANES), F32)],
        scratch_shapes=[pltpu.VMEM((8, LANES), F32)],
        compiler_params=_cparams(("arbitrary",)),
        name="route",
    )(logits)


def _table_lookup(idx, table):
    n = table.shape[0]
    return jnp.sum(jnp.where(idx[:, None] == jnp.arange(n, dtype=idx.dtype)[None, :], table[None, :], 0), axis=1)


def _dispatch_plan(route, cnt):
    T = route.shape[0]
    counts = cnt[0, N_EXPERT_GROUPS:N_EXPERT_GROUPS + N_EXPERTS].astype(jnp.int32)
    padded = (counts + MOE_BLOCK - 1) // MOE_BLOCK * MOE_BLOCK
    pad_end = jnp.cumsum(padded)
    pad_start = pad_end - padded
    start = jnp.cumsum(counts) - counts
    e0 = route[:, R_E0].astype(jnp.int32)
    e1 = route[:, R_E1].astype(jnp.int32)
    dest0 = _table_lookup(e0, pad_start) + route[:, R_RANK0].astype(jnp.int32)
    dest1 = _table_lookup(e1, pad_start) + route[:, R_RANK1].astype(jnp.int32)
    n_blocks = -(-(T * TOP_K) // MOE_BLOCK) + N_EXPERTS
    block_end = pad_end // MOE_BLOCK
    blk = jnp.arange(n_blocks, dtype=jnp.int32)
    block_expert = jnp.minimum(jnp.sum(blk[:, None] >= block_end[None, :], axis=1), N_EXPERTS - 1).astype(jnp.int32)
    n_used = block_end[-1:].astype(jnp.int32)
    order = jnp.argsort(jnp.stack([e0, e1], axis=1).reshape(-1), stable=True).astype(jnp.int32)
    shift = _table_lookup(block_expert, start - pad_start)
    pos = jnp.arange(n_blocks * MOE_BLOCK, dtype=jnp.int32) + jnp.repeat(shift, MOE_BLOCK)
    src_tok = order[jnp.clip(pos, 0, T * TOP_K - 1)] // TOP_K
    return dest0, dest1, src_tok, block_expert, n_used


def _expert_kernel(be_ref, nu_ref, x_ref, w1_ref, w3_ref, w2_ref, o_ref, w1b_ref, w3b_ref, w2b_ref):
    i = pl.program_id(0)
    last = nu_ref[0] - 1
    cur = be_ref[jnp.minimum(i, last)]
    prev = be_ref[jnp.minimum(jnp.maximum(i - 1, 0), last)]

    @pl.when((i == 0) | (cur != prev))
    def _():
        w1b_ref[...] = w1_ref[...].astype(BF16)
        w3b_ref[...] = w3_ref[...].astype(BF16)
        w2b_ref[...] = w2_ref[...].astype(BF16)

    @pl.when(i <= last)
    def _():
        x = x_ref[...]
        h1 = _dot(x, w1b_ref[...])
        h3 = _dot(x, w3b_ref[...])
        a = (h1 * _sigmoid(h1) * h3).astype(BF16)
        o_ref[...] = _dot(a, w2b_ref[...])


def _experts(buf, block_expert, n_used, w1, w3, w2):
    R, D = buf.shape
    FF = w1.shape[-1]
    nb = R // MOE_BLOCK
    blk = lambda i, be, nu: (jnp.minimum(i, nu[0] - 1), 0)
    wsel = lambda i, be, nu: (be[jnp.minimum(i, nu[0] - 1)], 0, 0)
    return pl.pallas_call(
        _expert_kernel,
        grid_spec=pltpu.PrefetchScalarGridSpec(
            num_scalar_prefetch=2,
            grid=(nb,),
            in_specs=[
                pl.BlockSpec((MOE_BLOCK, D), blk),
                pl.BlockSpec((None, D, FF), wsel),
                pl.BlockSpec((None, D, FF), wsel),
                pl.BlockSpec((None, FF, D), wsel),
            ],
            out_specs=pl.BlockSpec((MOE_BLOCK, D), blk),
            scratch_shapes=[pltpu.VMEM((D, FF), BF16), pltpu.VMEM((D, FF), BF16), pltpu.VMEM((FF, D), BF16)],
        ),
        out_shape=jax.ShapeDtypeStruct((R, D), F32),
        compiler_params=_cparams(("arbitrary",)),
        name="expert_mlp",
    )(block_expert, n_used, buf, w1, w3, w2)


def _final_kernel(x_ref, r0_ref, r1_ref, rt_ref, g_ref, o_ref):
    o_ref[...] = _rms(_moe_combine(x_ref, r0_ref, r1_ref, rt_ref), g_ref[...])


def _final_norm(x2, moe, g):
    T, D = x2.shape
    tm = min(ROW_TILE, T)
    row = lambda n: pl.BlockSpec((tm, n), lambda i: (i, 0))
    return pl.pallas_call(
        _final_kernel,
        grid=(T // tm,),
        in_specs=[row(D), row(D), row(D), row(LANES), pl.BlockSpec((1, D), lambda i: (0, 0))],
        out_specs=row(D),
        out_shape=jax.ShapeDtypeStruct((T, D), F32),
        compiler_params=_cparams(("parallel",)),
        name="final_norm",
    )(x2, *moe, g)


def _pack_plan(D):
    widths = (D_ATTN, D_ATTN, D_ATTN, ATTN_HEADS, D_SSD, CONV_CH, SSD_HEADS, D, D)
    src = dict(zip(("q", "k", "v", "f", "z", "xbc", "dt", "ga", "gs"),
                   zip(np.concatenate([[0], np.cumsum(widths)[:-1]]).tolist(), widths)))
    plan, dst = [], 0
    for name in ("q", "k", "v", "z", "xbc", "ga", "gs", "f", "dt"):
        s0, w = src[name]
        plan.append((s0, w, dst, ATTN_HEAD_DIM ** -0.5 * LOG2E if name == "q" else 1.0))
        dst += w
    return plan, dst


def _pack_kernel(w_ref, o_ref, *, plan, used):
    for s0, w, d0, scale in plan:
        piece = w_ref[:, s0:s0 + w]
        if scale != 1.0:
            piece = piece * scale
        o_ref[:, d0:d0 + w] = piece.astype(BF16)
    o_ref[:, used:] = jnp.zeros((o_ref.shape[0], o_ref.shape[1] - used), BF16)


def _pack_w_in(w_in):
    depth, D, cols = w_in.shape
    plan, used = _pack_plan(D)
    n_out = -(-used // LANES) * LANES
    rows = 128
    return pl.pallas_call(
        functools.partial(_pack_kernel, plan=plan, used=used),
        grid=(depth, D // rows),
        in_specs=[pl.BlockSpec((None, rows, cols), lambda l, r: (l, r, 0))],
        out_specs=pl.BlockSpec((None, rows, n_out), lambda l, r: (l, r, 0)),
        out_shape=jax.ShapeDtypeStruct((depth, D, n_out), BF16),
        compiler_params=_cparams(("parallel", "parallel")),
        name="pack_w_in",
    )(w_in)


def _small_rows(mat, lane0):
    depth, n = mat.shape
    return jnp.zeros((depth, 1, SMALL_W), F32).at[:, 0, lane0:lane0 + n].set(mat.astype(F32))


def kernel(x, norm_mix_g, w_in, b_f, conv_w, conv_b, dt_bias, a_log, d_skip, ssd_norm_g, w_br_attn, w_br_ssd,
           w_out, norm_ffn_g, w_group_router, b_group_router, w_expert_router, b_expert_router, w1, w3, w2,
           final_g):
    B, S, D = x.shape
    T = B * S
    depth = w_in.shape[0]
    w_packed = _pack_w_in(w_in)
    bf_rows = _small_rows(b_f, 0)
    dtb_rows = _small_rows(dt_bias, HEAD_LANE0)
    alog_rows = _small_rows(a_log, HEAD_LANE0)
    dskip_rows = jnp.repeat(d_skip, SSD_HEAD_DIM, axis=1)[:, None, :]
    router_pad = LANES - N_EXPERT_GROUPS - N_EXPERTS
    w_router = jnp.concatenate([w_group_router, w_expert_router, jnp.zeros((depth, D, router_pad), F32)], axis=2)
    b_router = jnp.concatenate([b_group_router, b_expert_router, jnp.zeros((depth, router_pad), F32)],
                               axis=1)[:, None, :]
    wa, ws, wo = w_br_attn.astype(BF16), w_br_ssd.astype(BF16), w_out.astype(BF16)

    def trunk(xb):
        Bb = xb.shape[0]
        Tb = Bb * S
        x2 = xb.reshape(Tb, D)
        moe = None
        for l in range(depth):
            x2, (qkv, z, xbc, gates, small) = _inproj(x2, moe, norm_mix_g[l][None, :], w_packed[l])
            small3 = small.reshape(Bb, S, SMALL_W)
            fk = _fcum(small3, bf_rows[l])
            y_attn = _attention(qkv.reshape(Bb, S, 3 * D_ATTN), fk).reshape(Tb, D_ATTN)
            y_ssd = _ssd(xbc.reshape(Bb, S, CONV_CH), z.reshape(Bb, S, D_SSD), small3, conv_w[l],
                         conv_b[l][None, :], dtb_rows[l], alog_rows[l], dskip_rows[l],
                         ssd_norm_g[l][None, :]).reshape(Tb, D_SSD)
            x2, h2, logits = _merge(y_attn, y_ssd, gates, x2, wa[l], ws[l], wo[l], norm_ffn_g[l][None, :],
                                    w_router[l], b_router[l])
            route, cnt = _route(logits)
            dest0, dest1, src_tok, block_expert, n_used = _dispatch_plan(route, cnt)
            out = _experts(h2[src_tok], block_expert, n_used, w1[l], w3[l], w2[l])
            moe = (out[dest0], out[dest1], route)
        return _final_norm(x2, moe, final_g[None, :]).reshape(Bb, S, D)

    halves = BATCH_SPLIT if B % BATCH_SPLIT == 0 else 1
    step = B // halves
    return jnp.concatenate([trunk(x[h * step:(h + 1) * step]) for h in range(halves)], axis=0)
```

```python
import functools

import numpy as np
import jax
import jax.numpy as jnp
from jax import lax
from jax.experimental import pallas as pl
from jax.experimental.pallas import tpu as pltpu

F32 = jnp.float32
BF16 = jnp.bfloat16

ATTN_HEADS = 8
ATTN_HEAD_DIM = 64
D_ATTN = ATTN_HEADS * ATTN_HEAD_DIM
SSD_HEADS = 16
SSD_HEAD_DIM = 64
D_SSD = SSD_HEADS * SSD_HEAD_DIM
SSD_GROUPS = 2
SSD_STATE = 128
CONV_WIDTH = 4
CONV_CH = D_SSD + 2 * SSD_GROUPS * SSD_STATE
N_EXPERT_GROUPS = 4
EXPERTS_PER_GROUP = 8
N_EXPERTS = N_EXPERT_GROUPS * EXPERTS_PER_GROUP
TOP_K = 2
EPS = 1e-6

LANES = 128
HEAD_LANE0 = 8
SMALL_W = LANES
NEG_BIG = -1e30
LOG2E = 1.4426950408889634
VMEM_LIMIT = 56 * 1024 * 1024

SSD_CHUNK = 128
ATTN_TILE = 512
FCUM_TILE = 1024
ROW_TILE = 512
MOE_BLOCK = 512
R_E0, R_E1, R_G0, R_G1, R_RANK0, R_RANK1 = range(6)


def _cparams(sem):
    return pltpu.CompilerParams(dimension_semantics=sem, vmem_limit_bytes=VMEM_LIMIT)


def _split3(x):
    hi = x.astype(BF16)
    r = x - hi.astype(F32)
    mid = r.astype(BF16)
    lo = (r - mid.astype(F32)).astype(BF16)
    return hi, mid, lo


def _dot(a, b):
    return jnp.dot(a, b, preferred_element_type=F32)


def _dot_nt(a, b):
    return lax.dot_general(a, b, (((1,), (1,)), ((), ())), preferred_element_type=F32)


def _dot_x_01(x, m01):
    hi, mid, lo = _split3(x)
    return _dot(hi, m01) + _dot(mid, m01) + _dot(lo, m01)


def _dot_01_x(m01, x):
    hi, mid, lo = _split3(x)
    return _dot(m01, hi) + _dot(m01, mid) + _dot(m01, lo)


def _tril01(n, strict=False):
    r = lax.broadcasted_iota(jnp.int32, (n, n), 0)
    c = lax.broadcasted_iota(jnp.int32, (n, n), 1)
    return ((c < r) if strict else (c <= r)).astype(BF16)


def _softplus(x):
    return jnp.maximum(x, 0.0) + jnp.log1p(jnp.exp(-jnp.abs(x)))


def _sigmoid(x):
    return 0.5 * jnp.tanh(0.5 * x) + 0.5


def _rms(x, g):
    ms = jnp.mean(x * x, axis=-1, keepdims=True)
    return x * lax.rsqrt(ms + EPS) * g


def _moe_combine(x_ref, r0_ref, r1_ref, rt_ref):
    rt = rt_ref[...]
    return x_ref[...] + (rt[:, R_G0:R_G0 + 1] * r0_ref[...] + rt[:, R_G1:R_G1 + 1] * r1_ref[...])


def _inproj_kernel(*refs, with_moe):
    if with_moe:
        x_ref, r0_ref, r1_ref, rt_ref, g_ref, w_ref, xo_ref, *outs = refs
        x = _moe_combine(x_ref, r0_ref, r1_ref, rt_ref)
        xo_ref[...] = x
    else:
        x_ref, g_ref, w_ref, *outs = refs
        x = x_ref[...]
    h = _rms(x, g_ref[...]).astype(BF16)
    off = 0
    for ref in outs:
        n = ref.shape[-1]
        for c0 in range(0, n, 512):
            c1 = min(c0 + 512, n)
            ref[:, c0:c1] = _dot(h, w_ref[:, off + c0:off + c1]).astype(ref.dtype)
        off += n


def _inproj(x2, moe, g, w_packed, layer):
    T, D = x2.shape
    widths = (3 * D_ATTN, D_SSD, CONV_CH, 2 * D, SMALL_W)
    dtypes = (BF16, F32, F32, F32, F32)
    tm = min(ROW_TILE, T)
    row = lambda n: pl.BlockSpec((tm, n), lambda i: (i, 0))
    with_moe = moe is not None
    ins = [x2] + (list(moe) if with_moe else []) + [g, w_packed]
    in_specs = [row(D)] + ([row(D), row(D), row(LANES)] if with_moe else []) + [
        pl.BlockSpec((1, D), lambda i: (0, 0)),
        pl.BlockSpec((None,) + w_packed.shape[1:], lambda i: (layer, 0, 0), pipeline_mode=pl.Buffered(1)),
    ]
    out_specs = ([row(D)] if with_moe else []) + [row(n) for n in widths]
    out_shape = ([jax.ShapeDtypeStruct((T, D), F32)] if with_moe else []) + [
        jax.ShapeDtypeStruct((T, n), dt) for n, dt in zip(widths, dtypes)]
    res = pl.pallas_call(
        functools.partial(_inproj_kernel, with_moe=with_moe),
        grid=(T // tm,),
        in_specs=in_specs,
        out_specs=out_specs,
        out_shape=out_shape,
        compiler_params=_cparams(("parallel",)),
        name="inproj",
    )(*ins)
    return (res[0], res[1:]) if with_moe else (x2, res)


N_FPARTS = 3
HEAD_PAIRS = D_ATTN // LANES


def _fcum_kernel(s_ref, b_ref, place_ref, fk_ref, carry_ref):
    c = pl.program_id(1)

    @pl.when(c == 0)
    def _():
        carry_ref[...] = jnp.zeros_like(carry_ref)

    x = s_ref[...] + b_ref[...]
    logf = jnp.minimum(x, 0.0) - jnp.log1p(jnp.exp(-jnp.abs(x)))
    n = x.shape[0]
    cum = _dot_01_x(_tril01(n), logf) + carry_ref[0:1, :]
    carry_ref[0:1, :] = cum[n - 1:n, :]
    parts = jnp.concatenate(_split3(cum * (-LOG2E)), axis=1)
    for p in range(HEAD_PAIRS):
        fk_ref[p] = _dot(parts, place_ref[p]).astype(BF16)


def _fpart_placement():
    m = np.zeros((HEAD_PAIRS, N_FPARTS * LANES, LANES), np.float32)
    for p in range(HEAD_PAIRS):
        for j in range(N_FPARTS):
            m[p, j * LANES + 2 * p, ATTN_HEAD_DIM + j] = 1.0
            m[p, j * LANES + 2 * p + 1, j] = 1.0
    return jnp.asarray(m, BF16)


def _fcum(small3, bf_row):
    B, S, _ = small3.shape
    L = min(FCUM_TILE, S)
    place = _fpart_placement()
    return pl.pallas_call(
        _fcum_kernel,
        grid=(B, S // L),
        in_specs=[
            pl.BlockSpec((None, L, SMALL_W), lambda b, c: (b, c, 0)),
            pl.BlockSpec((1, SMALL_W), lambda b, c: (0, 0)),
            pl.BlockSpec(place.shape, lambda b, c: (0, 0, 0)),
        ],
        out_specs=pl.BlockSpec((None, HEAD_PAIRS, L, LANES), lambda b, c: (b, 0, c, 0)),
        out_shape=jax.ShapeDtypeStruct((B, HEAD_PAIRS, S, LANES), BF16),
        scratch_shapes=[pltpu.VMEM((8, SMALL_W), F32)],
        compiler_params=_cparams(("parallel", "arbitrary")),
        name="forget_cumsum",
    )(small3, bf_row, place)


def _attn_kernel(q_ref, k_ref, v_ref, fk_ref, o_ref, ka_ref, kb_ref, va_ref, vb_ref, s_ref, p_ref, m_ref,
                 acc_ref, *, t):
    i = pl.program_id(2)
    lane = lax.broadcasted_iota(jnp.int32, (1, LANES), 1)
    lo = lane < ATTN_HEAD_DIM
    den_lane = (ATTN_HEAD_DIM, 0)
    ones_at = lambda cond: jnp.where(cond, 1.0, 0.0).astype(BF16)

    @pl.when(i == 0)
    def _():
        k = k_ref[...]
        fk = fk_ref[...]
        ka_ref[...] = jnp.where(lo, k, fk)
        kb_ref[...] = jnp.where(lo, fk, k)
        v = v_ref[...]
        va_ref[...] = jnp.where(lo, v, ones_at(lane == den_lane[0]))
        vb_ref[...] = jnp.where(lo, ones_at(lane == den_lane[1]), v)

    q = q_ref[...]
    bias_a = ones_at((lane >= ATTN_HEAD_DIM) & (lane < ATTN_HEAD_DIM + N_FPARTS))
    bias_b = ones_at(lane < N_FPARTS)
    q_heads = (jnp.where(lo, q, bias_a), jnp.where(lo, bias_b, q))
    k_refs = (ka_ref, kb_ref)
    v_refs = (va_ref, vb_ref)
    m_ref[...] = jnp.full(m_ref.shape, NEG_BIG, F32)
    acc_ref[...] = jnp.zeros(acc_ref.shape, F32)

    def step(kb, masked):
        ks = pl.multiple_of(kb * t, t)
        for e in range(2):
            s = _dot_nt(q_heads[e], k_refs[e][pl.ds(ks, t), :])
            if masked:
                r = lax.broadcasted_iota(jnp.int32, (t, t), 0)
                c = lax.broadcasted_iota(jnp.int32, (t, t), 1)
                s = jnp.where(c <= r, s, NEG_BIG)
            s_ref[e] = s
        for e in range(2):
            m_prev = m_ref[e]
            m_new = jnp.maximum(m_prev, jnp.max(s_ref[e], axis=-1, keepdims=True))
            m_ref[e] = m_new
            p_ref[e] = jnp.exp2(s_ref[e] - jnp.concatenate([m_new] * (t // LANES), axis=1)).astype(BF16)
            pv = _dot(p_ref[e], v_refs[e][pl.ds(ks, t), :])
            acc_ref[e] = acc_ref[e] * jnp.exp2(m_prev - m_new) + pv

    def body(kb, carry):
        step(kb, False)
        return carry

    lax.fori_loop(0, i, body, 0)
    step(i, True)
    acc_a = acc_ref[0]
    acc_b = acc_ref[1]
    inv_a = 1.0 / acc_a[:, den_lane[0]:den_lane[0] + 1]
    inv_b = 1.0 / acc_b[:, den_lane[1]:den_lane[1] + 1]
    o_ref[...] = jnp.where(lo, acc_a * inv_a, acc_b * inv_b).astype(o_ref.dtype)


def _attention(qkv3, fk):
    B, S, _ = qkv3.shape
    t = min(ATTN_TILE, S)
    seq = lambda: pltpu.VMEM((S, LANES), BF16)
    return pl.pallas_call(
        functools.partial(_attn_kernel, t=t),
        grid=(B, HEAD_PAIRS, S // t),
        in_specs=[
            pl.BlockSpec((None, t, LANES), lambda b, h, i: (b, i, h)),
            pl.BlockSpec((None, S, LANES), lambda b, h, i: (b, 0, HEAD_PAIRS + h)),
            pl.BlockSpec((None, S, LANES), lambda b, h, i: (b, 0, 2 * HEAD_PAIRS + h)),
            pl.BlockSpec((None, None, S, LANES), lambda b, h, i: (b, h, 0, 0)),
        ],
        out_specs=pl.BlockSpec((None, t, LANES), lambda b, h, i: (b, i, h)),
        out_shape=jax.ShapeDtypeStruct((B, S, D_ATTN), BF16),
        scratch_shapes=[
            seq(), seq(), seq(), seq(),
            pltpu.VMEM((2, t, t), F32), pltpu.VMEM((2, t, t), BF16),
            pltpu.VMEM((2, t, LANES), F32), pltpu.VMEM((2, t, LANES), F32),
        ],
        compiler_params=_cparams(("parallel", "parallel", "arbitrary")),
        name="fox_attention",
    )(qkv3, qkv3, qkv3, fk)


def _ssd_kernel(xbc_ref, z_ref, small_ref, cw_ref, cb_ref, dtb_ref, alog_ref, dskip_ref, ng_ref,
                e1_ref, o_ref, xbuf_ref, state_ref, *, L):
    c = pl.program_id(1)
    halo = 8

    @pl.when(c == 0)
    def _():
        xbuf_ref[0:halo, :] = jnp.zeros((halo, CONV_CH), F32)
        state_ref[...] = jnp.zeros_like(state_ref)

    @pl.when(c > 0)
    def _():
        xbuf_ref[0:halo, :] = xbuf_ref[L:L + halo, :]

    xbuf_ref[halo:halo + L, :] = xbc_ref[...]
    w = cw_ref[...]
    conv = cb_ref[...] + w[3:4, :] * xbuf_ref[halo:halo + L, :]
    for kk in range(CONV_WIDTH - 1):
        conv = conv + w[kk:kk + 1, :] * xbuf_ref[pl.ds(halo - (CONV_WIDTH - 1) + kk, L), :]
    u = conv * _sigmoid(conv)
    xs = u[:, :D_SSD]
    bm = u[:, D_SSD:D_SSD + SSD_GROUPS * SSD_STATE]
    cm = u[:, D_SSD + SSD_GROUPS * SSD_STATE:]

    dt = _softplus(small_ref[...] + dtb_ref[...])
    a = dt * (-jnp.exp(alog_ref[...]))
    acs = _dot_01_x(_tril01(L), a)
    e1 = e1_ref[...]
    dt_e = _dot_x_01(dt, e1)
    acs_e = _dot_x_01(acs, e1)
    acs_t = acs.T
    last_e = acs_e[L - 1:L, :]
    x_dt = xs * dt_e
    x_dt_bf = x_dt.astype(BF16)
    x_end_bf = (x_dt * jnp.exp(last_e - acs_e)).astype(BF16)
    grow = jnp.exp(acs_e)
    chunk_decay = jnp.exp(last_e)

    row = lax.broadcasted_iota(jnp.int32, (L, L), 0)
    col = lax.broadcasted_iota(jnp.int32, (L, L), 1)
    causal = col <= row
    lo = lax.broadcasted_iota(jnp.int32, (1, LANES), 1) < SSD_HEAD_DIM
    pairs_per_group = SSD_HEADS // SSD_GROUPS // 2

    ys = []
    for g in range(SSD_GROUPS):
        bg = bm[:, g * SSD_STATE:(g + 1) * SSD_STATE]
        bg_bf = bg.astype(BF16)
        bg_t_bf = bg.T.astype(BF16)
        cg_bf = cm[:, g * SSD_STATE:(g + 1) * SSD_STATE].astype(BF16)
        cb = _dot_nt(cg_bf, bg_bf)
        for jj in range(pairs_per_group):
            j = g * pairs_per_group + jj
            sl = slice(j * LANES, (j + 1) * LANES)
            xp = x_dt_bf[:, sl]
            yd = []
            for e in range(2):
                h = 2 * j + e
                hl = HEAD_LANE0 + h
                seg = jnp.broadcast_to(acs[:, hl:hl + 1], (L, L)) - acs_t[hl:hl + 1, :]
                dec = jnp.exp(jnp.where(causal, seg, NEG_BIG))
                yd.append(_dot((cb * dec).astype(BF16), xp))
            st = state_ref[j]
            y_off = _dot(cg_bf, st.astype(BF16)) * grow[:, sl]
            ys.append(jnp.where(lo, yd[0], yd[1]) + y_off)
            state_ref[j] = st * chunk_decay[:, sl] + _dot(bg_t_bf, x_end_bf[:, sl])

    y = jnp.concatenate(ys, axis=1) + xs * dskip_ref[...]
    zz = z_ref[...]
    y = y * (zz * _sigmoid(zz))
    o_ref[...] = _rms(y, ng_ref[...]).astype(o_ref.dtype)


def _head_expand_mat():
    e1 = np.zeros((LANES, D_SSD), np.float32)
    for h in range(SSD_HEADS):
        e1[HEAD_LANE0 + h, h * SSD_HEAD_DIM:(h + 1) * SSD_HEAD_DIM] = 1.0
    return jnp.asarray(e1, BF16)


def _ssd(xbc3, z3, small3, conv_w, conv_b, dtb_row, alog_row, dskip_row, norm_g):
    B, S, _ = xbc3.shape
    L = min(SSD_CHUNK, S)
    e1 = _head_expand_mat()
    const = lambda shape: pl.BlockSpec(shape, lambda b, c: (0,) * len(shape))
    return pl.pallas_call(
        functools.partial(_ssd_kernel, L=L),
        grid=(B, S // L),
        in_specs=[
            pl.BlockSpec((None, L, CONV_CH), lambda b, c: (b, c, 0)),
            pl.BlockSpec((None, L, D_SSD), lambda b, c: (b, c, 0)),
            pl.BlockSpec((None, L, SMALL_W), lambda b, c: (b, c, 0)),
            const((CONV_WIDTH, CONV_CH)), const((1, CONV_CH)), const((1, SMALL_W)), const((1, SMALL_W)),
            const((1, D_SSD)), const((1, D_SSD)), const(e1.shape),
        ],
        out_specs=pl.BlockSpec((None, L, D_SSD), lambda b, c: (b, c, 0)),
        out_shape=jax.ShapeDtypeStruct((B, S, D_SSD), BF16),
        scratch_shapes=[pltpu.VMEM((L + 8, CONV_CH), F32),
                        pltpu.VMEM((SSD_HEADS // 2, SSD_STATE, LANES), F32)],
        compiler_params=_cparams(("parallel", "arbitrary")),
        name="conv_ssd",
    )(xbc3, z3, small3, conv_w, conv_b, dtb_row, alog_row, dskip_row, norm_g, e1)


def _merge_kernel(ya_ref, ys_ref, gates_ref, x_ref, wa_ref, ws_ref, wo_ref, g2_ref, wr_ref, br_ref,
                  xo_ref, h_ref, lg_ref):
    D = x_ref.shape[-1]
    pa = _dot(ya_ref[...], wa_ref[...])
    ps = _dot(ys_ref[...], ws_ref[...])
    merged = _sigmoid(gates_ref[:, :D]) * pa + _sigmoid(gates_ref[:, D:]) * ps
    xn = x_ref[...] + _dot(merged.astype(BF16), wo_ref[...])
    xo_ref[...] = xn
    h = _rms(xn, g2_ref[...])
    h_ref[...] = h.astype(BF16)
    h_hi, h_mid, _ = _split3(h)
    w_hi, w_mid, _ = _split3(wr_ref[...])
    lg_ref[...] = _dot(h_hi, w_hi) + _dot(h_hi, w_mid) + _dot(h_mid, w_hi) + br_ref[...]


def _merge(ya, ys, gates, x2, wa, ws, wo, g2, wr, br):
    T, D = x2.shape
    tm = min(ROW_TILE, T)
    row = lambda n: pl.BlockSpec((tm, n), lambda i: (i, 0))
    const = lambda a: pl.BlockSpec(a.shape, lambda i: (0, 0))
    return pl.pallas_call(
        _merge_kernel,
        grid=(T // tm,),
        in_specs=[row(D_ATTN), row(D_SSD), row(2 * D), row(D),
                  const(wa), const(ws), const(wo), const(g2), const(wr), const(br)],
        out_specs=[row(D), row(D), row(LANES)],
        out_shape=[jax.ShapeDtypeStruct((T, D), F32), jax.ShapeDtypeStruct((T, D), BF16),
                   jax.ShapeDtypeStruct((T, LANES), F32)],
        compiler_params=_cparams(("parallel",)),
        name="merge_outproj_router",
    )(ya, ys, gates, x2, wa, ws, wo, g2, wr, br)


def _route_kernel(lg_ref, route_ref, cnt_ref, carry_ref):
    i = pl.program_id(0)

    @pl.when(i == 0)
    def _():
        carry_ref[...] = jnp.zeros_like(carry_ref)

    lg = lg_ref[...]
    tm = lg.shape[0]
    lane_i = lax.broadcasted_iota(jnp.int32, (tm, LANES), 1)
    lane = lane_i.astype(F32)
    first = lambda cond: jnp.min(jnp.where(cond, lane, float(LANES)), axis=-1, keepdims=True)

    gmask = lane_i < N_EXPERT_GROUPS
    gl = jnp.where(gmask, lg, NEG_BIG)
    gmax = jnp.max(gl, axis=-1, keepdims=True)
    gsum = jnp.sum(jnp.where(gmask, jnp.exp(gl - gmax), 0.0), axis=-1, keepdims=True)
    g_idx = first(gmask & (gl == gmax))
    g_w = 1.0 / gsum

    e_lo = N_EXPERT_GROUPS + g_idx * EXPERTS_PER_GROUP
    emask = (lane >= e_lo) & (lane < e_lo + EXPERTS_PER_GROUP)
    el = jnp.where(emask, lg, NEG_BIG)
    emax = jnp.max(el, axis=-1, keepdims=True)
    esum = jnp.sum(jnp.where(emask, jnp.exp(el - emax), 0.0), axis=-1, keepdims=True)
    l0 = first(emask & (el == emax))
    el2 = jnp.where(lane == l0, NEG_BIG, el)
    m2 = jnp.max(el2, axis=-1, keepdims=True)
    l1 = first(emask & (lane != l0) & (el2 == m2))
    gate0 = g_w * (1.0 / esum)
    gate1 = g_w * (jnp.exp(m2 - emax) / esum)

    is0 = lane == l0
    is1 = lane == l1
    onehot = jnp.where(is0 | is1, 1.0, 0.0)
    before = _dot(_tril01(tm, strict=True), onehot.astype(BF16)) + carry_ref[0:1, :]
    rank0 = jnp.sum(jnp.where(is0, before, 0.0), axis=-1, keepdims=True)
    rank1 = jnp.sum(jnp.where(is1, before, 0.0), axis=-1, keepdims=True)
    total = carry_ref[0:1, :] + jnp.sum(onehot, axis=0, keepdims=True)
    carry_ref[0:1, :] = total
    cnt_ref[...] = jnp.broadcast_to(total, cnt_ref.shape)

    rec = jnp.zeros((tm, LANES), F32)
    for ln, val in ((R_E0, l0 - N_EXPERT_GROUPS), (R_E1, l1 - N_EXPERT_GROUPS), (R_G0, gate0), (R_G1, gate1),
                    (R_RANK0, rank0), (R_RANK1, rank1)):
        rec = jnp.where(lane_i == ln, val, rec)
    route_ref[...] = rec


def _route(logits):
    T = logits.shape[0]
    tm = min(ROW_TILE, T)
    return pl.pallas_call(
        _route_kernel,
        grid=(T // tm,),
        in_specs=[pl.BlockSpec((tm, LANES), lambda i: (i, 0))],
        out_specs=[pl.BlockSpec((tm, LANES), lambda i: (i, 0)), pl.BlockSpec((8, LANES), lambda i: (0, 0))],
        out_shape=[jax.ShapeDtypeStruct((T, LANES), F32), jax.ShapeDtypeStruct((8, LANES), F32)],
        scratch_shapes=[pltpu.VMEM((8, LANES), F32)],
        compiler_params=_cparams(("arbitrary",)),
        name="route",
    )(logits)


def _table_lookup(idx, table):
    n = table.shape[0]
    return jnp.sum(jnp.where(idx[:, None] == jnp.arange(n, dtype=idx.dtype)[None, :], table[None, :], 0), axis=1)


def _dispatch_plan(route, cnt):
    T = route.shape[0]
    counts = cnt[0, N_EXPERT_GROUPS:N_EXPERT_GROUPS + N_EXPERTS].astype(jnp.int32)
    padded = (counts + MOE_BLOCK - 1) // MOE_BLOCK * MOE_BLOCK
    pad_end = jnp.cumsum(padded)
    pad_start = pad_end - padded
    start = jnp.cumsum(counts) - counts
    e0 = route[:, R_E0].astype(jnp.int32)
    e1 = route[:, R_E1].astype(jnp.int32)
    dest0 = _table_lookup(e0, pad_start) + route[:, R_RANK0].astype(jnp.int32)
    dest1 = _table_lookup(e1, pad_start) + route[:, R_RANK1].astype(jnp.int32)
    n_blocks = -(-(T * TOP_K) // MOE_BLOCK) + N_EXPERTS
    block_end = pad_end // MOE_BLOCK
    blk = jnp.arange(n_blocks, dtype=jnp.int32)
    block_expert = jnp.minimum(jnp.sum(blk[:, None] >= block_end[None, :], axis=1), N_EXPERTS - 1).astype(jnp.int32)
    n_used = block_end[-1:].astype(jnp.int32)
    order = jnp.argsort(jnp.stack([e0, e1], axis=1).reshape(-1), stable=True).astype(jnp.int32)
    shift = _table_lookup(block_expert, start - pad_start)
    pos = jnp.arange(n_blocks * MOE_BLOCK, dtype=jnp.int32) + jnp.repeat(shift, MOE_BLOCK)
    src_tok = order[jnp.clip(pos, 0, T * TOP_K - 1)] // TOP_K
    return dest0, dest1, src_tok, block_expert, n_used


def _expert_kernel(be_ref, nu_ref, x_ref, w1_ref, w3_ref, w2_ref, o_ref, w1b_ref, w3b_ref, w2b_ref):
    i = pl.program_id(0)
    last = nu_ref[0] - 1
    cur = be_ref[jnp.minimum(i, last)]
    prev = be_ref[jnp.minimum(jnp.maximum(i - 1, 0), last)]

    @pl.when((i == 0) | (cur != prev))
    def _():
        w1b_ref[...] = w1_ref[...].astype(BF16)
        w3b_ref[...] = w3_ref[...].astype(BF16)
        w2b_ref[...] = w2_ref[...].astype(BF16)

    @pl.when(i <= last)
    def _():
        x = x_ref[...]
        h1 = _dot(x, w1b_ref[...])
        h3 = _dot(x, w3b_ref[...])
        a = (h1 * _sigmoid(h1) * h3).astype(BF16)
        o_ref[...] = _dot(a, w2b_ref[...])


def _experts(buf, block_expert, n_used, w1, w3, w2, layer):
    R, D = buf.shape
    FF = w1.shape[-1]
    nb = R // MOE_BLOCK
    blk = lambda i, be, nu: (jnp.minimum(i, nu[0] - 1), 0)
    wsel = lambda i, be, nu: (layer, be[jnp.minimum(i, nu[0] - 1)], 0, 0)
    return pl.pallas_call(
        _expert_kernel,
        grid_spec=pltpu.PrefetchScalarGridSpec(
            num_scalar_prefetch=2,
            grid=(nb,),
            in_specs=[
                pl.BlockSpec((MOE_BLOCK, D), blk),
                pl.BlockSpec((None, None, D, FF), wsel),
                pl.BlockSpec((None, None, D, FF), wsel),
                pl.BlockSpec((None, None, FF, D), wsel),
            ],
            out_specs=pl.BlockSpec((MOE_BLOCK, D), blk),
            scratch_shapes=[pltpu.VMEM((D, FF), BF16), pltpu.VMEM((D, FF), BF16), pltpu.VMEM((FF, D), BF16)],
        ),
        out_shape=jax.ShapeDtypeStruct((R, D), F32),
        compiler_params=_cparams(("arbitrary",)),
        name="expert_mlp",
    )(block_expert, n_used, buf, w1, w3, w2)


def _final_kernel(x_ref, r0_ref, r1_ref, rt_ref, g_ref, o_ref):
    o_ref[...] = _rms(_moe_combine(x_ref, r0_ref, r1_ref, rt_ref), g_ref[...])


def _final_norm(x2, moe, g):
    T, D = x2.shape
    tm = min(ROW_TILE, T)
    row = lambda n: pl.BlockSpec((tm, n), lambda i: (i, 0))
    return pl.pallas_call(
        _final_kernel,
        grid=(T // tm,),
        in_specs=[row(D), row(D), row(D), row(LANES), pl.BlockSpec((1, D), lambda i: (0, 0))],
        out_specs=row(D),
        out_shape=jax.ShapeDtypeStruct((T, D), F32),
        compiler_params=_cparams(("parallel",)),
        name="final_norm",
    )(x2, *moe, g)


def _pack_plan(D):
    widths = (D_ATTN, D_ATTN, D_ATTN, ATTN_HEADS, D_SSD, CONV_CH, SSD_HEADS, D, D)
    src = dict(zip(("q", "k", "v", "f", "z", "xbc", "dt", "ga", "gs"),
                   zip(np.concatenate([[0], np.cumsum(widths)[:-1]]).tolist(), widths)))
    plan, dst = [], 0
    for name in ("q", "k", "v", "z", "xbc", "ga", "gs", "f", "dt"):
        s0, w = src[name]
        plan.append((s0, w, dst, ATTN_HEAD_DIM ** -0.5 * LOG2E if name == "q" else 1.0))
        dst += w
    return plan, dst


def _pack_kernel(w_ref, o_ref, *, plan, used):
    for s0, w, d0, scale in plan:
        piece = w_ref[:, s0:s0 + w]
        if scale != 1.0:
            piece = piece * scale
        o_ref[:, d0:d0 + w] = piece.astype(BF16)
    o_ref[:, used:] = jnp.zeros((o_ref.shape[0], o_ref.shape[1] - used), BF16)


def _pack_w_in(w_in):
    depth, D, cols = w_in.shape
    plan, used = _pack_plan(D)
    n_out = -(-used // LANES) * LANES
    rows = 128
    return pl.pallas_call(
        functools.partial(_pack_kernel, plan=plan, used=used),
        grid=(depth, D // rows),
        in_specs=[pl.BlockSpec((None, rows, cols), lambda l, r: (l, r, 0))],
        out_specs=pl.BlockSpec((None, rows, n_out), lambda l, r: (l, r, 0)),
        out_shape=jax.ShapeDtypeStruct((depth, D, n_out), BF16),
        compiler_params=_cparams(("parallel", "parallel")),
        name="pack_w_in",
    )(w_in)


def _small_rows(mat, lane0):
    depth, n = mat.shape
    return jnp.zeros((depth, 1, SMALL_W), F32).at[:, 0, lane0:lane0 + n].set(mat.astype(F32))


def kernel(x, norm_mix_g, w_in, b_f, conv_w, conv_b, dt_bias, a_log, d_skip, ssd_norm_g, w_br_attn, w_br_ssd,
           w_out, norm_ffn_g, w_group_router, b_group_router, w_expert_router, b_expert_router, w1, w3, w2,
           final_g):
    B, S, D = x.shape
    T = B * S
    depth = w_in.shape[0]
    w_packed = _pack_w_in(w_in)
    bf_rows = _small_rows(b_f, 0)
    dtb_rows = _small_rows(dt_bias, HEAD_LANE0)
    alog_rows = _small_rows(a_log, HEAD_LANE0)
    dskip_rows = jnp.repeat(d_skip, SSD_HEAD_DIM, axis=1)[:, None, :]
    router_pad = LANES - N_EXPERT_GROUPS - N_EXPERTS
    w_router = jnp.concatenate([w_group_router, w_expert_router, jnp.zeros((depth, D, router_pad), F32)], axis=2)
    b_router = jnp.concatenate([b_group_router, b_expert_router, jnp.zeros((depth, router_pad), F32)],
                               axis=1)[:, None, :]
    wa, ws, wo = w_br_attn.astype(BF16), w_br_ssd.astype(BF16), w_out.astype(BF16)

    def trunk(xb):
        Bb = xb.shape[0]
        Tb = Bb * S
        x2 = xb.reshape(Tb, D)
        moe = None
        for l in range(depth):
            x2, (qkv, z, xbc, gates, small) = _inproj(x2, moe, norm_mix_g[l][None, :], w_packed, l)
            small3 = small.reshape(Bb, S, SMALL_W)
            fk = _fcum(small3, bf_rows[l])
            y_attn = _attention(qkv.reshape(Bb, S, 3 * D_ATTN), fk).reshape(Tb, D_ATTN)
            y_ssd = _ssd(xbc.reshape(Bb, S, CONV_CH), z.reshape(Bb, S, D_SSD), small3, conv_w[l],
                         conv_b[l][None, :], dtb_rows[l], alog_rows[l], dskip_rows[l],
                         ssd_norm_g[l][None, :]).reshape(Tb, D_SSD)
            x2, h2, logits = _merge(y_attn, y_ssd, gates, x2, wa[l], ws[l], wo[l], norm_ffn_g[l][None, :],
                                    w_router[l], b_router[l])
            route, cnt = _route(logits)
            dest0, dest1, src_tok, block_expert, n_used = _dispatch_plan(route, cnt)
            out = _experts(h2[src_tok], block_expert, n_used, w1, w3, w2, l)
            moe = (out[dest0], out[dest1], route)
        return _final_norm(x2, moe, final_g[None, :]).reshape(Bb, S, D)

    return trunk(x)
```

```python
import functools

import numpy as np
import jax
import jax.numpy as jnp
from jax import lax
from jax.experimental import pallas as pl
from jax.experimental.pallas import tpu as pltpu

F32 = jnp.float32
BF16 = jnp.bfloat16

ATTN_HEADS = 8
ATTN_HEAD_DIM = 64
D_ATTN = ATTN_HEADS * ATTN_HEAD_DIM
SSD_HEADS = 16
SSD_HEAD_DIM = 64
D_SSD = SSD_HEADS * SSD_HEAD_DIM
SSD_GROUPS = 2
SSD_STATE = 128
CONV_WIDTH = 4
CONV_CH = D_SSD + 2 * SSD_GROUPS * SSD_STATE
N_EXPERT_GROUPS = 4
EXPERTS_PER_GROUP = 8
N_EXPERTS = N_EXPERT_GROUPS * EXPERTS_PER_GROUP
TOP_K = 2
EPS = 1e-6

LANES = 128
HEAD_LANE0 = 8
SMALL_W = LANES
NEG_BIG = -1e30
LOG2E = 1.4426950408889634
VMEM_LIMIT = 56 * 1024 * 1024

SSD_CHUNK = 128
ATTN_TILE = 1024
FCUM_TILE = 512
ROW_TILE = 512
MOE_BLOCK = 256
R_E0, R_E1, R_G0, R_G1, R_RANK0, R_RANK1 = range(6)


def _cparams(sem):
    return pltpu.CompilerParams(dimension_semantics=sem, vmem_limit_bytes=VMEM_LIMIT)


def _split3(x):
    hi = x.astype(BF16)
    r = x - hi.astype(F32)
    mid = r.astype(BF16)
    lo = (r - mid.astype(F32)).astype(BF16)
    return hi, mid, lo


def _dot(a, b):
    return jnp.dot(a, b, preferred_element_type=F32)


def _dot_nt(a, b):
    return lax.dot_general(a, b, (((1,), (1,)), ((), ())), preferred_element_type=F32)


def _dot_x_01(x, m01):
    hi, mid, lo = _split3(x)
    return _dot(hi, m01) + _dot(mid, m01) + _dot(lo, m01)


def _dot_01_x(m01, x):
    hi, mid, lo = _split3(x)
    return _dot(m01, hi) + _dot(m01, mid) + _dot(m01, lo)


def _tril01(n, strict=False):
    r = lax.broadcasted_iota(jnp.int32, (n, n), 0)
    c = lax.broadcasted_iota(jnp.int32, (n, n), 1)
    return ((c < r) if strict else (c <= r)).astype(BF16)


def _softplus(x):
    return jnp.maximum(x, 0.0) + jnp.log1p(jnp.exp(-jnp.abs(x)))


def _sigmoid(x):
    return 0.5 * jnp.tanh(0.5 * x) + 0.5


def _rms(x, g):
    ms = jnp.mean(x * x, axis=-1, keepdims=True)
    return x * lax.rsqrt(ms + EPS) * g


def _moe_combine(x_ref, r0_ref, r1_ref, rt_ref):
    rt = rt_ref[...]
    return x_ref[...] + (rt[:, R_G0:R_G0 + 1] * r0_ref[...] + rt[:, R_G1:R_G1 + 1] * r1_ref[...])


def _inproj_kernel(*refs, with_moe):
    if with_moe:
        x_ref, r0_ref, r1_ref, rt_ref, g_ref, w_ref, xo_ref, *outs = refs
        x = _moe_combine(x_ref, r0_ref, r1_ref, rt_ref)
        xo_ref[...] = x
    else:
        x_ref, g_ref, w_ref, *outs = refs
        x = x_ref[...]
    h = _rms(x, g_ref[...]).astype(BF16)
    off = 0
    for ref in outs:
        n = ref.shape[-1]
        for c0 in range(0, n, 512):
            c1 = min(c0 + 512, n)
            ref[:, c0:c1] = _dot(h, w_ref[:, off + c0:off + c1]).astype(ref.dtype)
        off += n


def _inproj(x2, moe, g, w_packed, layer):
    T, D = x2.shape
    widths = (3 * D_ATTN, D_SSD, CONV_CH, 2 * D, SMALL_W)
    dtypes = (BF16, F32, F32, F32, F32)
    tm = min(ROW_TILE, T)
    row = lambda n: pl.BlockSpec((tm, n), lambda i: (i, 0))
    with_moe = moe is not None
    ins = [x2] + (list(moe) if with_moe else []) + [g, w_packed]
    in_specs = [row(D)] + ([row(D), row(D), row(LANES)] if with_moe else []) + [
        pl.BlockSpec((1, D), lambda i: (0, 0)),
        pl.BlockSpec((None,) + w_packed.shape[1:], lambda i: (layer, 0, 0), pipeline_mode=pl.Buffered(1)),
    ]
    out_specs = ([row(D)] if with_moe else []) + [row(n) for n in widths]
    out_shape = ([jax.ShapeDtypeStruct((T, D), F32)] if with_moe else []) + [
        jax.ShapeDtypeStruct((T, n), dt) for n, dt in zip(widths, dtypes)]
    res = pl.pallas_call(
        functools.partial(_inproj_kernel, with_moe=with_moe),
        grid=(T // tm,),
        in_specs=in_specs,
        out_specs=out_specs,
        out_shape=out_shape,
        compiler_params=_cparams(("parallel",)),
        name="inproj",
    )(*ins)
    return (res[0], res[1:]) if with_moe else (x2, res)


N_FPARTS = 3
HEAD_PAIRS = D_ATTN // LANES


def _fcum_kernel(s_ref, b_ref, place_ref, fk_ref, carry_ref):
    c = pl.program_id(1)

    @pl.when(c == 0)
    def _():
        carry_ref[...] = jnp.zeros_like(carry_ref)

    x = s_ref[...] + b_ref[...]
    logf = jnp.minimum(x, 0.0) - jnp.log1p(jnp.exp(-jnp.abs(x)))
    n = x.shape[0]
    cum = _dot_01_x(_tril01(n), logf) + carry_ref[0:1, :]
    carry_ref[0:1, :] = cum[n - 1:n, :]
    parts = jnp.concatenate(_split3(cum * (-LOG2E)), axis=1)
    for p in range(HEAD_PAIRS):
        fk_ref[p] = _dot(parts, place_ref[p]).astype(BF16)


def _fpart_placement():
    m = np.zeros((HEAD_PAIRS, N_FPARTS * LANES, LANES), np.float32)
    for p in range(HEAD_PAIRS):
        for j in range(N_FPARTS):
            m[p, j * LANES + 2 * p, ATTN_HEAD_DIM + j] = 1.0
            m[p, j * LANES + 2 * p + 1, j] = 1.0
    return jnp.asarray(m, BF16)


def _fcum(small3, bf_row):
    B, S, _ = small3.shape
    L = min(FCUM_TILE, S)
    place = _fpart_placement()
    return pl.pallas_call(
        _fcum_kernel,
        grid=(B, S // L),
        in_specs=[
            pl.BlockSpec((None, L, SMALL_W), lambda b, c: (b, c, 0)),
            pl.BlockSpec((1, SMALL_W), lambda b, c: (0, 0)),
            pl.BlockSpec(place.shape, lambda b, c: (0, 0, 0)),
        ],
        out_specs=pl.BlockSpec((None, HEAD_PAIRS, L, LANES), lambda b, c: (b, 0, c, 0)),
        out_shape=jax.ShapeDtypeStruct((B, HEAD_PAIRS, S, LANES), BF16),
        scratch_shapes=[pltpu.VMEM((8, SMALL_W), F32)],
        compiler_params=_cparams(("parallel", "arbitrary")),
        name="forget_cumsum",
    )(small3, bf_row, place)


def _attn_kernel(q_ref, k_ref, v_ref, fk_ref, o_ref, ka_ref, kb_ref, va_ref, vb_ref, s_ref, p_ref, m_ref,
                 acc_ref, *, t):
    i = pl.program_id(2)
    lane = lax.broadcasted_iota(jnp.int32, (1, LANES), 1)
    lo = lane < ATTN_HEAD_DIM
    den_lane = (ATTN_HEAD_DIM, 0)
    ones_at = lambda cond: jnp.where(cond, 1.0, 0.0).astype(BF16)

    @pl.when(i == 0)
    def _():
        k = k_ref[...]
        fk = fk_ref[...]
        ka_ref[...] = jnp.where(lo, k, fk)
        kb_ref[...] = jnp.where(lo, fk, k)
        v = v_ref[...]
        va_ref[...] = jnp.where(lo, v, ones_at(lane == den_lane[0]))
        vb_ref[...] = jnp.where(lo, ones_at(lane == den_lane[1]), v)

    q = q_ref[...]
    bias_a = ones_at((lane >= ATTN_HEAD_DIM) & (lane < ATTN_HEAD_DIM + N_FPARTS))
    bias_b = ones_at(lane < N_FPARTS)
    q_heads = (jnp.where(lo, q, bias_a), jnp.where(lo, bias_b, q))
    k_refs = (ka_ref, kb_ref)
    v_refs = (va_ref, vb_ref)
    m_ref[...] = jnp.full(m_ref.shape, NEG_BIG, F32)
    acc_ref[...] = jnp.zeros(acc_ref.shape, F32)

    def step(kb, r0, nr, nc, masked):
        ks = pl.multiple_of(kb * t, t)
        rows = slice(r0, r0 + nr)
        for e in range(2):
            s = _dot_nt(q_heads[e][rows], k_refs[e][pl.ds(ks, nc), :])
            if masked:
                r = r0 + lax.broadcasted_iota(jnp.int32, (nr, nc), 0)
                c = lax.broadcasted_iota(jnp.int32, (nr, nc), 1)
                s = jnp.where(c <= r, s, NEG_BIG)
            s_ref[e, rows, 0:nc] = s
        for e in range(2):
            m_prev = m_ref[e, rows, :]
            m_new = jnp.maximum(m_prev, jnp.max(s_ref[e, rows, 0:nc], axis=-1, keepdims=True))
            m_ref[e, rows, :] = m_new
            p_ref[e, rows, 0:nc] = jnp.exp2(
                s_ref[e, rows, 0:nc] - jnp.concatenate([m_new] * (nc // LANES), axis=1)).astype(BF16)
            pv = _dot(p_ref[e, rows, 0:nc], v_refs[e][pl.ds(ks, nc), :])
            acc_ref[e, rows, :] = acc_ref[e, rows, :] * jnp.exp2(m_prev - m_new) + pv

    def body(kb, carry):
        step(kb, 0, t, t, False)
        return carry

    lax.fori_loop(0, i, body, 0)
    half = t // 2
    step(i, 0, half, half, True)
    step(i, half, half, t, True)
    acc_a = acc_ref[0]
    acc_b = acc_ref[1]
    inv_a = 1.0 / acc_a[:, den_lane[0]:den_lane[0] + 1]
    inv_b = 1.0 / acc_b[:, den_lane[1]:den_lane[1] + 1]
    o_ref[...] = jnp.where(lo, acc_a * inv_a, acc_b * inv_b).astype(o_ref.dtype)


def _attention(qkv3, fk):
    B, S, _ = qkv3.shape
    t = min(ATTN_TILE, S)
    seq = lambda: pltpu.VMEM((S, LANES), BF16)
    return pl.pallas_call(
        functools.partial(_attn_kernel, t=t),
        grid=(B, HEAD_PAIRS, S // t),
        in_specs=[
            pl.BlockSpec((None, t, LANES), lambda b, h, i: (b, i, h)),
            pl.BlockSpec((None, S, LANES), lambda b, h, i: (b, 0, HEAD_PAIRS + h)),
            pl.BlockSpec((None, S, LANES), lambda b, h, i: (b, 0, 2 * HEAD_PAIRS + h)),
            pl.BlockSpec((None, None, S, LANES), lambda b, h, i: (b, h, 0, 0)),
        ],
        out_specs=pl.BlockSpec((None, t, LANES), lambda b, h, i: (b, i, h)),
        out_shape=jax.ShapeDtypeStruct((B, S, D_ATTN), BF16),
        scratch_shapes=[
            seq(), seq(), seq(), seq(),
            pltpu.VMEM((2, t, t), F32), pltpu.VMEM((2, t, t), BF16),
            pltpu.VMEM((2, t, LANES), F32), pltpu.VMEM((2, t, LANES), F32),
        ],
        compiler_params=_cparams(("parallel", "parallel", "arbitrary")),
        name="fox_attention",
    )(qkv3, qkv3, qkv3, fk)


def _ssd_kernel(xbc_ref, z_ref, small_ref, cw_ref, cb_ref, dtb_ref, alog_ref, dskip_ref, ng_ref,
                e1_ref, o_ref, xbuf_ref, state_ref, *, L):
    c = pl.program_id(1)
    halo = 8

    @pl.when(c == 0)
    def _():
        xbuf_ref[0:halo, :] = jnp.zeros((halo, CONV_CH), F32)
        state_ref[...] = jnp.zeros_like(state_ref)

    @pl.when(c > 0)
    def _():
        xbuf_ref[0:halo, :] = xbuf_ref[L:L + halo, :]

    xbuf_ref[halo:halo + L, :] = xbc_ref[...]
    w = cw_ref[...]
    conv = cb_ref[...] + w[3:4, :] * xbuf_ref[halo:halo + L, :]
    for kk in range(CONV_WIDTH - 1):
        conv = conv + w[kk:kk + 1, :] * xbuf_ref[pl.ds(halo - (CONV_WIDTH - 1) + kk, L), :]
    u = conv * _sigmoid(conv)
    xs = u[:, :D_SSD]
    bm = u[:, D_SSD:D_SSD + SSD_GROUPS * SSD_STATE]
    cm = u[:, D_SSD + SSD_GROUPS * SSD_STATE:]

    dt = _softplus(small_ref[...] + dtb_ref[...])
    a = dt * (-jnp.exp(alog_ref[...]))
    acs = _dot_01_x(_tril01(L), a)
    e1 = e1_ref[...]
    dt_e = _dot_x_01(dt, e1)
    acs_e = _dot_x_01(acs, e1)
    acs_t = acs.T
    last_e = acs_e[L - 1:L, :]
    x_dt = xs * dt_e
    x_dt_bf = x_dt.astype(BF16)
    x_end_bf = (x_dt * jnp.exp(last_e - acs_e)).astype(BF16)
    grow = jnp.exp(acs_e)
    chunk_decay = jnp.exp(last_e)

    row = lax.broadcasted_iota(jnp.int32, (L, L), 0)
    col = lax.broadcasted_iota(jnp.int32, (L, L), 1)
    causal = col <= row
    lo = lax.broadcasted_iota(jnp.int32, (1, LANES), 1) < SSD_HEAD_DIM
    pairs_per_group = SSD_HEADS // SSD_GROUPS // 2

    ys = []
    for g in range(SSD_GROUPS):
        bg = bm[:, g * SSD_STATE:(g + 1) * SSD_STATE]
        bg_bf = bg.astype(BF16)
        bg_t_bf = bg.T.astype(BF16)
        cg_bf = cm[:, g * SSD_STATE:(g + 1) * SSD_STATE].astype(BF16)
        cb = _dot_nt(cg_bf, bg_bf)
        for jj in range(pairs_per_group):
            j = g * pairs_per_group + jj
            sl = slice(j * LANES, (j + 1) * LANES)
            xp = x_dt_bf[:, sl]
            yd = []
            for e in range(2):
                h = 2 * j + e
                hl = HEAD_LANE0 + h
                seg = jnp.broadcast_to(acs[:, hl:hl + 1], (L, L)) - acs_t[hl:hl + 1, :]
                dec = jnp.exp(jnp.where(causal, seg, NEG_BIG))
                yd.append(_dot((cb * dec).astype(BF16), xp))
            st = state_ref[j]
            y_off = _dot(cg_bf, st.astype(BF16)) * grow[:, sl]
            ys.append(jnp.where(lo, yd[0], yd[1]) + y_off)
            state_ref[j] = st * chunk_decay[:, sl] + _dot(bg_t_bf, x_end_bf[:, sl])

    y = jnp.concatenate(ys, axis=1) + xs * dskip_ref[...]
    zz = z_ref[...]
    y = y * (zz * _sigmoid(zz))
    o_ref[...] = _rms(y, ng_ref[...]).astype(o_ref.dtype)


def _head_expand_mat():
    e1 = np.zeros((LANES, D_SSD), np.float32)
    for h in range(SSD_HEADS):
        e1[HEAD_LANE0 + h, h * SSD_HEAD_DIM:(h + 1) * SSD_HEAD_DIM] = 1.0
    return jnp.asarray(e1, BF16)


def _ssd(xbc3, z3, small3, conv_w, conv_b, dtb_row, alog_row, dskip_row, norm_g):
    B, S, _ = xbc3.shape
    L = min(SSD_CHUNK, S)
    e1 = _head_expand_mat()
    const = lambda shape: pl.BlockSpec(shape, lambda b, c: (0,) * len(shape))
    return pl.pallas_call(
        functools.partial(_ssd_kernel, L=L),
        grid=(B, S // L),
        in_specs=[
            pl.BlockSpec((None, L, CONV_CH), lambda b, c: (b, c, 0)),
            pl.BlockSpec((None, L, D_SSD), lambda b, c: (b, c, 0)),
            pl.BlockSpec((None, L, SMALL_W), lambda b, c: (b, c, 0)),
            const((CONV_WIDTH, CONV_CH)), const((1, CONV_CH)), const((1, SMALL_W)), const((1, SMALL_W)),
            const((1, D_SSD)), const((1, D_SSD)), const(e1.shape),
        ],
        out_specs=pl.BlockSpec((None, L, D_SSD), lambda b, c: (b, c, 0)),
        out_shape=jax.ShapeDtypeStruct((B, S, D_SSD), BF16),
        scratch_shapes=[pltpu.VMEM((L + 8, CONV_CH), F32),
                        pltpu.VMEM((SSD_HEADS // 2, SSD_STATE, LANES), F32)],
        compiler_params=_cparams(("parallel", "arbitrary")),
        name="conv_ssd",
    )(xbc3, z3, small3, conv_w, conv_b, dtb_row, alog_row, dskip_row, norm_g, e1)


def _merge_kernel(ya_ref, ys_ref, gates_ref, x_ref, wa_ref, ws_ref, wo_ref, g2_ref, wr_ref, br_ref,
                  xo_ref, h_ref, lg_ref):
    D = x_ref.shape[-1]
    pa = _dot(ya_ref[...], wa_ref[...])
    ps = _dot(ys_ref[...], ws_ref[...])
    merged = _sigmoid(gates_ref[:, :D]) * pa + _sigmoid(gates_ref[:, D:]) * ps
    xn = x_ref[...] + _dot(merged.astype(BF16), wo_ref[...])
    xo_ref[...] = xn
    h = _rms(xn, g2_ref[...])
    h_ref[...] = h.astype(BF16)
    h_hi, h_mid, _ = _split3(h)
    w_hi, w_mid, _ = _split3(wr_ref[...])
    lg_ref[...] = _dot(h_hi, w_hi) + _dot(h_hi, w_mid) + _dot(h_mid, w_hi) + br_ref[...]


def _merge(ya, ys, gates, x2, wa, ws, wo, g2, wr, br):
    T, D = x2.shape
    tm = min(ROW_TILE, T)
    row = lambda n: pl.BlockSpec((tm, n), lambda i: (i, 0))
    const = lambda a: pl.BlockSpec(a.shape, lambda i: (0, 0))
    return pl.pallas_call(
        _merge_kernel,
        grid=(T // tm,),
        in_specs=[row(D_ATTN), row(D_SSD), row(2 * D), row(D),
                  const(wa), const(ws), const(wo), const(g2), const(wr), const(br)],
        out_specs=[row(D), row(D), row(LANES)],
        out_shape=[jax.ShapeDtypeStruct((T, D), F32), jax.ShapeDtypeStruct((T, D), BF16),
                   jax.ShapeDtypeStruct((T, LANES), F32)],
        compiler_params=_cparams(("parallel",)),
        name="merge_outproj_router",
    )(ya, ys, gates, x2, wa, ws, wo, g2, wr, br)


def _route_kernel(lg_ref, route_ref, cnt_ref, carry_ref):
    i = pl.program_id(0)

    @pl.when(i == 0)
    def _():
        carry_ref[...] = jnp.zeros_like(carry_ref)

    lg = lg_ref[...]
    tm = lg.shape[0]
    lane_i = lax.broadcasted_iota(jnp.int32, (tm, LANES), 1)
    lane = lane_i.astype(F32)
    first = lambda cond: jnp.min(jnp.where(cond, lane, float(LANES)), axis=-1, keepdims=True)

    gmask = lane_i < N_EXPERT_GROUPS
    gl = jnp.where(gmask, lg, NEG_BIG)
    gmax = jnp.max(gl, axis=-1, keepdims=True)
    gsum = jnp.sum(jnp.where(gmask, jnp.exp(gl - gmax), 0.0), axis=-1, keepdims=True)
    g_idx = first(gmask & (gl == gmax))
    g_w = 1.0 / gsum

    e_lo = N_EXPERT_GROUPS + g_idx * EXPERTS_PER_GROUP
    emask = (lane >= e_lo) & (lane < e_lo + EXPERTS_PER_GROUP)
    el = jnp.where(emask, lg, NEG_BIG)
    emax = jnp.max(el, axis=-1, keepdims=True)
    esum = jnp.sum(jnp.where(emask, jnp.exp(el - emax), 0.0), axis=-1, keepdims=True)
    l0 = first(emask & (el == emax))
    el2 = jnp.where(lane == l0, NEG_BIG, el)
    m2 = jnp.max(el2, axis=-1, keepdims=True)
    l1 = first(emask & (lane != l0) & (el2 == m2))
    gate0 = g_w * (1.0 / esum)
    gate1 = g_w * (jnp.exp(m2 - emax) / esum)

    is0 = lane == l0
    is1 = lane == l1
    onehot = jnp.where(is0 | is1, 1.0, 0.0)
    before = _dot(_tril01(tm, strict=True), onehot.astype(BF16)) + carry_ref[0:1, :]
    rank0 = jnp.sum(jnp.where(is0, before, 0.0), axis=-1, keepdims=True)
    rank1 = jnp.sum(jnp.where(is1, before, 0.0), axis=-1, keepdims=True)
    total = carry_ref[0:1, :] + jnp.sum(onehot, axis=0, keepdims=True)
    carry_ref[0:1, :] = total
    cnt_ref[...] = jnp.broadcast_to(total, cnt_ref.shape)

    rec = jnp.zeros((tm, LANES), F32)
    for ln, val in ((R_E0, l0 - N_EXPERT_GROUPS), (R_E1, l1 - N_EXPERT_GROUPS), (R_G0, gate0), (R_G1, gate1),
                    (R_RANK0, rank0), (R_RANK1, rank1)):
        rec = jnp.where(lane_i == ln, val, rec)
    route_ref[...] = rec


def _route(logits):
    T = logits.shape[0]
    tm = min(ROW_TILE, T)
    return pl.pallas_call(
        _route_kernel,
        grid=(T // tm,),
        in_specs=[pl.BlockSpec((tm, LANES), lambda i: (i, 0))],
        out_specs=[pl.BlockSpec((tm, LANES), lambda i: (i, 0)), pl.BlockSpec((8, LANES), lambda i: (0, 0))],
        out_shape=[jax.ShapeDtypeStruct((T, LANES), F32), jax.ShapeDtypeStruct((8, LANES), F32)],
        scratch_shapes=[pltpu.VMEM((8, LANES), F32)],
        compiler_params=_cparams(("arbitrary",)),
        name="route",
    )(logits)


def _table_lookup(idx, table):
    n = table.shape[0]
    return jnp.sum(jnp.where(idx[:, None] == jnp.arange(n, dtype=idx.dtype)[None, :], table[None, :], 0), axis=1)


def _dispatch_plan(route, cnt):
    T = route.shape[0]
    counts = cnt[0, N_EXPERT_GROUPS:N_EXPERT_GROUPS + N_EXPERTS].astype(jnp.int32)
    padded = (counts + MOE_BLOCK - 1) // MOE_BLOCK * MOE_BLOCK
    pad_end = jnp.cumsum(padded)
    pad_start = pad_end - padded
    start = jnp.cumsum(counts) - counts
    e0 = route[:, R_E0].astype(jnp.int32)
    e1 = route[:, R_E1].astype(jnp.int32)
    dest0 = _table_lookup(e0, pad_start) + route[:, R_RANK0].astype(jnp.int32)
    dest1 = _table_lookup(e1, pad_start) + route[:, R_RANK1].astype(jnp.int32)
    n_blocks = -(-(T * TOP_K) // MOE_BLOCK) + N_EXPERTS
    block_end = pad_end // MOE_BLOCK
    blk = jnp.arange(n_blocks, dtype=jnp.int32)
    block_expert = jnp.minimum(jnp.sum(blk[:, None] >= block_end[None, :], axis=1), N_EXPERTS - 1).astype(jnp.int32)
    n_used = block_end[-1:].astype(jnp.int32)
    order = jnp.argsort(jnp.stack([e0, e1], axis=1).reshape(-1), stable=True).astype(jnp.int32)
    shift = _table_lookup(block_expert, start - pad_start)
    pos = jnp.arange(n_blocks * MOE_BLOCK, dtype=jnp.int32) + jnp.repeat(shift, MOE_BLOCK)
    src_tok = order[jnp.clip(pos, 0, T * TOP_K - 1)] // TOP_K
    return dest0, dest1, src_tok, block_expert, n_used


def _expert_kernel(be_ref, nu_ref, x_ref, w1_ref, w3_ref, w2_ref, o_ref, w1b_ref, w3b_ref, w2b_ref):
    i = pl.program_id(0)
    last = nu_ref[0] - 1
    cur = be_ref[jnp.minimum(i, last)]
    prev = be_ref[jnp.minimum(jnp.maximum(i - 1, 0), last)]

    @pl.when((i == 0) | (cur != prev))
    def _():
        w1b_ref[...] = w1_ref[...].astype(BF16)
        w3b_ref[...] = w3_ref[...].astype(BF16)
        w2b_ref[...] = w2_ref[...].astype(BF16)

    @pl.when(i <= last)
    def _():
        x = x_ref[...]
        h1 = _dot(x, w1b_ref[...])
        h3 = _dot(x, w3b_ref[...])
        a = (h1 * _sigmoid(h1) * h3).astype(BF16)
        o_ref[...] = _dot(a, w2b_ref[...])


def _experts(buf, block_expert, n_used, w1, w3, w2, layer):
    R, D = buf.shape
    FF = w1.shape[-1]
    nb = R // MOE_BLOCK
    blk = lambda i, be, nu: (jnp.minimum(i, nu[0] - 1), 0)
    wsel = lambda i, be, nu: (layer, be[jnp.minimum(i, nu[0] - 1)], 0, 0)
    return pl.pallas_call(
        _expert_kernel,
        grid_spec=pltpu.PrefetchScalarGridSpec(
            num_scalar_prefetch=2,
            grid=(nb,),
            in_specs=[
                pl.BlockSpec((MOE_BLOCK, D), blk),
                pl.BlockSpec((None, None, D, FF), wsel),
                pl.BlockSpec((None, None, D, FF), wsel),
                pl.BlockSpec((None, None, FF, D), wsel),
            ],
            out_specs=pl.BlockSpec((MOE_BLOCK, D), blk),
            scratch_shapes=[pltpu.VMEM((D, FF), BF16), pltpu.VMEM((D, FF), BF16), pltpu.VMEM((FF, D), BF16)],
        ),
        out_shape=jax.ShapeDtypeStruct((R, D), F32),
        compiler_params=_cparams(("arbitrary",)),
        name="expert_mlp",
    )(block_expert, n_used, buf, w1, w3, w2)


def _final_kernel(x_ref, r0_ref, r1_ref, rt_ref, g_ref, o_ref):
    o_ref[...] = _rms(_moe_combine(x_ref, r0_ref, r1_ref, rt_ref), g_ref[...])


def _final_norm(x2, moe, g):
    T, D = x2.shape
    tm = min(ROW_TILE, T)
    row = lambda n: pl.BlockSpec((tm, n), lambda i: (i, 0))
    return pl.pallas_call(
        _final_kernel,
        grid=(T // tm,),
        in_specs=[row(D), row(D), row(D), row(LANES), pl.BlockSpec((1, D), lambda i: (0, 0))],
        out_specs=row(D),
        out_shape=jax.ShapeDtypeStruct((T, D), F32),
        compiler_params=_cparams(("parallel",)),
        name="final_norm",
    )(x2, *moe, g)


def _pack_plan(D):
    widths = (D_ATTN, D_ATTN, D_ATTN, ATTN_HEADS, D_SSD, CONV_CH, SSD_HEADS, D, D)
    src = dict(zip(("q", "k", "v", "f", "z", "xbc", "dt", "ga", "gs"),
                   zip(np.concatenate([[0], np.cumsum(widths)[:-1]]).tolist(), widths)))
    plan, dst = [], 0
    for name in ("q", "k", "v", "z", "xbc", "ga", "gs", "f", "dt"):
        s0, w = src[name]
        plan.append((s0, w, dst, ATTN_HEAD_DIM ** -0.5 * LOG2E if name == "q" else 1.0))
        dst += w
    return plan, dst


def _pack_kernel(w_ref, o_ref, *, plan, used):
    for s0, w, d0, scale in plan:
        piece = w_ref[:, s0:s0 + w]
        if scale != 1.0:
            piece = piece * scale
        o_ref[:, d0:d0 + w] = piece.astype(BF16)
    o_ref[:, used:] = jnp.zeros((o_ref.shape[0], o_ref.shape[1] - used), BF16)


def _pack_w_in(w_in):
    depth, D, cols = w_in.shape
    plan, used = _pack_plan(D)
    n_out = -(-used // LANES) * LANES
    rows = 128
    return pl.pallas_call(
        functools.partial(_pack_kernel, plan=plan, used=used),
        grid=(depth, D // rows),
        in_specs=[pl.BlockSpec((None, rows, cols), lambda l, r: (l, r, 0))],
        out_specs=pl.BlockSpec((None, rows, n_out), lambda l, r: (l, r, 0)),
        out_shape=jax.ShapeDtypeStruct((depth, D, n_out), BF16),
        compiler_params=_cparams(("parallel", "parallel")),
        name="pack_w_in",
    )(w_in)


def _small_rows(mat, lane0):
    depth, n = mat.shape
    return jnp.zeros((depth, 1, SMALL_W), F32).at[:, 0, lane0:lane0 + n].set(mat.astype(F32))


def kernel(x, norm_mix_g, w_in, b_f, conv_w, conv_b, dt_bias, a_log, d_skip, ssd_norm_g, w_br_attn, w_br_ssd,
           w_out, norm_ffn_g, w_group_router, b_group_router, w_expert_router, b_expert_router, w1, w3, w2,
           final_g):
    B, S, D = x.shape
    T = B * S
    depth = w_in.shape[0]
    w_packed = _pack_w_in(w_in)
    bf_rows = _small_rows(b_f, 0)
    dtb_rows = _small_rows(dt_bias, HEAD_LANE0)
    alog_rows = _small_rows(a_log, HEAD_LANE0)
    dskip_rows = jnp.repeat(d_skip, SSD_HEAD_DIM, axis=1)[:, None, :]
    router_pad = LANES - N_EXPERT_GROUPS - N_EXPERTS
    w_router = jnp.concatenate([w_group_router, w_expert_router, jnp.zeros((depth, D, router_pad), F32)], axis=2)
    b_router = jnp.concatenate([b_group_router, b_expert_router, jnp.zeros((depth, router_pad), F32)],
                               axis=1)[:, None, :]
    wa, ws, wo = w_br_attn.astype(BF16), w_br_ssd.astype(BF16), w_out.astype(BF16)

    def trunk(xb):
        Bb = xb.shape[0]
        Tb = Bb * S
        x2 = xb.reshape(Tb, D)
        moe = None
        for l in range(depth):
            x2, (qkv, z, xbc, gates, small) = _inproj(x2, moe, norm_mix_g[l][None, :], w_packed, l)
            small3 = small.reshape(Bb, S, SMALL_W)
            fk = _fcum(small3, bf_rows[l])
            y_attn = _attention(qkv.reshape(Bb, S, 3 * D_ATTN), fk).reshape(Tb, D_ATTN)
            y_ssd = _ssd(xbc.reshape(Bb, S, CONV_CH), z.reshape(Bb, S, D_SSD), small3, conv_w[l],
                         conv_b[l][None, :], dtb_rows[l], alog_rows[l], dskip_rows[l],
                         ssd_norm_g[l][None, :]).reshape(Tb, D_SSD)
            x2, h2, logits = _merge(y_attn, y_ssd, gates, x2, wa[l], ws[l], wo[l], norm_ffn_g[l][None, :],
                                    w_router[l], b_router[l])
            route, cnt = _route(logits)
            dest0, dest1, src_tok, block_expert, n_used = _dispatch_plan(route, cnt)
            out = _experts(h2[src_tok], block_expert, n_used, w1, w3, w2, l)
            moe = (out[dest0], out[dest1], route)
        return _final_norm(x2, moe, final_g[None, :]).reshape(Bb, S, D)

    return trunk(x)
```

```python
import functools

import numpy as np
import jax
import jax.numpy as jnp
from jax import lax
from jax.experimental import pallas as pl
from jax.experimental.pallas import tpu as pltpu

F32 = jnp.float32
BF16 = jnp.bfloat16

ATTN_HEADS = 8
ATTN_HEAD_DIM = 64
D_ATTN = ATTN_HEADS * ATTN_HEAD_DIM
SSD_HEADS = 16
SSD_HEAD_DIM = 64
D_SSD = SSD_HEADS * SSD_HEAD_DIM
SSD_GROUPS = 2
SSD_STATE = 128
CONV_WIDTH = 4
CONV_CH = D_SSD + 2 * SSD_GROUPS * SSD_STATE
N_EXPERT_GROUPS = 4
EXPERTS_PER_GROUP = 8
N_EXPERTS = N_EXPERT_GROUPS * EXPERTS_PER_GROUP
TOP_K = 2
EPS = 1e-6

LANES = 128
HEAD_LANE0 = 8
SMALL_W = LANES
NEG_BIG = -1e30
LOG2E = 1.4426950408889634
VMEM_LIMIT = 56 * 1024 * 1024

SSD_CHUNK = 128
ATTN_TILE = 1024
FCUM_TILE = 512
ROW_TILE = 512
MOE_BLOCK = 256
R_E0, R_E1, R_G0, R_G1, R_RANK0, R_RANK1 = range(6)


def _cparams(sem):
    return pltpu.CompilerParams(dimension_semantics=sem, vmem_limit_bytes=VMEM_LIMIT)


def _split3(x):
    hi = x.astype(BF16)
    r = x - hi.astype(F32)
    mid = r.astype(BF16)
    lo = (r - mid.astype(F32)).astype(BF16)
    return hi, mid, lo


def _dot(a, b):
    return jnp.dot(a, b, preferred_element_type=F32)


def _dot_nt(a, b):
    return lax.dot_general(a, b, (((1,), (1,)), ((), ())), preferred_element_type=F32)


def _dot_x_01(x, m01):
    hi, mid, lo = _split3(x)
    return _dot(hi, m01) + _dot(mid, m01) + _dot(lo, m01)


def _dot_01_x(m01, x):
    hi, mid, lo = _split3(x)
    return _dot(m01, hi) + _dot(m01, mid) + _dot(m01, lo)


def _tril01(n, strict=False):
    r = lax.broadcasted_iota(jnp.int32, (n, n), 0)
    c = lax.broadcasted_iota(jnp.int32, (n, n), 1)
    return ((c < r) if strict else (c <= r)).astype(BF16)


def _softplus(x):
    return jnp.maximum(x, 0.0) + jnp.log1p(jnp.exp(-jnp.abs(x)))


def _sigmoid(x):
    return 0.5 * jnp.tanh(0.5 * x) + 0.5


def _rms(x, g):
    ms = jnp.mean(x * x, axis=-1, keepdims=True)
    return x * lax.rsqrt(ms + EPS) * g


def _moe_specs(T, tm, D):
    return [pl.BlockSpec((tm, D), lambda i: (i, 0)), pl.BlockSpec((tm, D), lambda i: (i + T // tm, 0)),
            pl.BlockSpec((tm, LANES), lambda i: (i, 0))]


def _moe_combine(x_ref, r0_ref, r1_ref, rt_ref):
    rt = rt_ref[...]
    return x_ref[...] + (rt[:, R_G0:R_G0 + 1] * r0_ref[...] + rt[:, R_G1:R_G1 + 1] * r1_ref[...])


def _inproj_kernel(*refs, with_moe):
    if with_moe:
        x_ref, r0_ref, r1_ref, rt_ref, g_ref, w_ref, xo_ref, *outs = refs
        x = _moe_combine(x_ref, r0_ref, r1_ref, rt_ref)
        xo_ref[...] = x
    else:
        x_ref, g_ref, w_ref, *outs = refs
        x = x_ref[...]
    h = _rms(x, g_ref[...]).astype(BF16)
    off = 0
    for ref in outs:
        n = ref.shape[-1]
        for c0 in range(0, n, 512):
            c1 = min(c0 + 512, n)
            ref[:, c0:c1] = _dot(h, w_ref[:, off + c0:off + c1]).astype(ref.dtype)
        off += n


def _inproj(x2, moe, g, w_packed, layer):
    T, D = x2.shape
    widths = (3 * D_ATTN, D_SSD, CONV_CH, 2 * D, SMALL_W)
    dtypes = (BF16, F32, F32, F32, F32)
    tm = min(ROW_TILE, T)
    row = lambda n: pl.BlockSpec((tm, n), lambda i: (i, 0))
    with_moe = moe is not None
    ins = [x2] + ([moe[0], moe[0], moe[1]] if with_moe else []) + [g, w_packed]
    in_specs = [row(D)] + (_moe_specs(T, tm, D) if with_moe else []) + [
        pl.BlockSpec((1, D), lambda i: (0, 0)),
        pl.BlockSpec((None,) + w_packed.shape[1:], lambda i: (layer, 0, 0), pipeline_mode=pl.Buffered(1)),
    ]
    out_specs = ([row(D)] if with_moe else []) + [row(n) for n in widths]
    out_shape = ([jax.ShapeDtypeStruct((T, D), F32)] if with_moe else []) + [
        jax.ShapeDtypeStruct((T, n), dt) for n, dt in zip(widths, dtypes)]
    res = pl.pallas_call(
        functools.partial(_inproj_kernel, with_moe=with_moe),
        grid=(T // tm,),
        in_specs=in_specs,
        out_specs=out_specs,
        out_shape=out_shape,
        compiler_params=_cparams(("parallel",)),
        name="inproj",
    )(*ins)
    return (res[0], res[1:]) if with_moe else (x2, res)


N_FPARTS = 3
HEAD_PAIRS = D_ATTN // LANES


def _fcum_kernel(s_ref, b_ref, place_ref, fk_ref, carry_ref):
    c = pl.program_id(1)

    @pl.when(c == 0)
    def _():
        carry_ref[...] = jnp.zeros_like(carry_ref)

    x = s_ref[...] + b_ref[...]
    logf = jnp.minimum(x, 0.0) - jnp.log1p(jnp.exp(-jnp.abs(x)))
    n = x.shape[0]
    cum = _dot_01_x(_tril01(n), logf) + carry_ref[0:1, :]
    carry_ref[0:1, :] = cum[n - 1:n, :]
    parts = jnp.concatenate(_split3(cum * (-LOG2E)), axis=1)
    for p in range(HEAD_PAIRS):
        fk_ref[p] = _dot(parts, place_ref[p]).astype(BF16)


def _fpart_placement():
    m = np.zeros((HEAD_PAIRS, N_FPARTS * LANES, LANES), np.float32)
    for p in range(HEAD_PAIRS):
        for j in range(N_FPARTS):
            m[p, j * LANES + 2 * p, ATTN_HEAD_DIM + j] = 1.0
            m[p, j * LANES + 2 * p + 1, j] = 1.0
    return jnp.asarray(m, BF16)


def _fcum(small3, bf_row):
    B, S, _ = small3.shape
    L = min(FCUM_TILE, S)
    place = _fpart_placement()
    return pl.pallas_call(
        _fcum_kernel,
        grid=(B, S // L),
        in_specs=[
            pl.BlockSpec((None, L, SMALL_W), lambda b, c: (b, c, 0)),
            pl.BlockSpec((1, SMALL_W), lambda b, c: (0, 0)),
            pl.BlockSpec(place.shape, lambda b, c: (0, 0, 0)),
        ],
        out_specs=pl.BlockSpec((None, HEAD_PAIRS, L, LANES), lambda b, c: (b, 0, c, 0)),
        out_shape=jax.ShapeDtypeStruct((B, HEAD_PAIRS, S, LANES), BF16),
        scratch_shapes=[pltpu.VMEM((8, SMALL_W), F32)],
        compiler_params=_cparams(("parallel", "arbitrary")),
        name="forget_cumsum",
    )(small3, bf_row, place)


def _attn_kernel(q_ref, k_ref, v_ref, fk_ref, o_ref, ka_ref, kb_ref, va_ref, vb_ref, s_ref, p_ref, m_ref,
                 acc_ref, *, t):
    i = pl.program_id(2)
    lane = lax.broadcasted_iota(jnp.int32, (1, LANES), 1)
    lo = lane < ATTN_HEAD_DIM
    den_lane = (ATTN_HEAD_DIM, 0)
    ones_at = lambda cond: jnp.where(cond, 1.0, 0.0).astype(BF16)

    @pl.when(i == 0)
    def _():
        k = k_ref[...]
        fk = fk_ref[...]
        ka_ref[...] = jnp.where(lo, k, fk)
        kb_ref[...] = jnp.where(lo, fk, k)
        v = v_ref[...]
        va_ref[...] = jnp.where(lo, v, ones_at(lane == den_lane[0]))
        vb_ref[...] = jnp.where(lo, ones_at(lane == den_lane[1]), v)

    q = q_ref[...]
    bias_a = ones_at((lane >= ATTN_HEAD_DIM) & (lane < ATTN_HEAD_DIM + N_FPARTS))
    bias_b = ones_at(lane < N_FPARTS)
    q_heads = (jnp.where(lo, q, bias_a), jnp.where(lo, bias_b, q))
    k_refs = (ka_ref, kb_ref)
    v_refs = (va_ref, vb_ref)
    m_ref[...] = jnp.full(m_ref.shape, NEG_BIG, F32)
    acc_ref[...] = jnp.zeros(acc_ref.shape, F32)

    def step(kb, r0, nr, nc, masked):
        ks = pl.multiple_of(kb * t, t)
        rows = slice(r0, r0 + nr)
        for e in range(2):
            s = _dot_nt(q_heads[e][rows], k_refs[e][pl.ds(ks, nc), :])
            if masked:
                r = r0 + lax.broadcasted_iota(jnp.int32, (nr, nc), 0)
                c = lax.broadcasted_iota(jnp.int32, (nr, nc), 1)
                s = jnp.where(c <= r, s, NEG_BIG)
            s_ref[e, rows, 0:nc] = s
        for e in range(2):
            m_prev = m_ref[e, rows, :]
            m_new = jnp.maximum(m_prev, jnp.max(s_ref[e, rows, 0:nc], axis=-1, keepdims=True))
            m_ref[e, rows, :] = m_new
            p_ref[e, rows, 0:nc] = jnp.exp2(
                s_ref[e, rows, 0:nc] - jnp.concatenate([m_new] * (nc // LANES), axis=1)).astype(BF16)
            pv = _dot(p_ref[e, rows, 0:nc], v_refs[e][pl.ds(ks, nc), :])
            acc_ref[e, rows, :] = acc_ref[e, rows, :] * jnp.exp2(m_prev - m_new) + pv

    def body(kb, carry):
        step(kb, 0, t, t, False)
        return carry

    lax.fori_loop(0, i, body, 0)
    half = t // 2
    step(i, 0, half, half, True)
    step(i, half, half, t, True)
    acc_a = acc_ref[0]
    acc_b = acc_ref[1]
    inv_a = 1.0 / acc_a[:, den_lane[0]:den_lane[0] + 1]
    inv_b = 1.0 / acc_b[:, den_lane[1]:den_lane[1] + 1]
    o_ref[...] = jnp.where(lo, acc_a * inv_a, acc_b * inv_b).astype(o_ref.dtype)


def _attention(qkv3, fk):
    B, S, _ = qkv3.shape
    t = min(ATTN_TILE, S)
    seq = lambda: pltpu.VMEM((S, LANES), BF16)
    return pl.pallas_call(
        functools.partial(_attn_kernel, t=t),
        grid=(B, HEAD_PAIRS, S // t),
        in_specs=[
            pl.BlockSpec((None, t, LANES), lambda b, h, i: (b, i, h)),
            pl.BlockSpec((None, S, LANES), lambda b, h, i: (b, 0, HEAD_PAIRS + h)),
            pl.BlockSpec((None, S, LANES), lambda b, h, i: (b, 0, 2 * HEAD_PAIRS + h)),
            pl.BlockSpec((None, None, S, LANES), lambda b, h, i: (b, h, 0, 0)),
        ],
        out_specs=pl.BlockSpec((None, t, LANES), lambda b, h, i: (b, i, h)),
        out_shape=jax.ShapeDtypeStruct((B, S, D_ATTN), BF16),
        scratch_shapes=[
            seq(), seq(), seq(), seq(),
            pltpu.VMEM((2, t, t), F32), pltpu.VMEM((2, t, t), BF16),
            pltpu.VMEM((2, t, LANES), F32), pltpu.VMEM((2, t, LANES), F32),
        ],
        compiler_params=_cparams(("parallel", "parallel", "arbitrary")),
        name="fox_attention",
    )(qkv3, qkv3, qkv3, fk)


def _ssd_kernel(xbc_ref, z_ref, small_ref, cw_ref, cb_ref, dtb_ref, alog_ref, dskip_ref, ng_ref,
                e1_ref, o_ref, xbuf_ref, state_ref, *, L):
    c = pl.program_id(1)
    halo = 8

    @pl.when(c == 0)
    def _():
        xbuf_ref[0:halo, :] = jnp.zeros((halo, CONV_CH), F32)
        state_ref[...] = jnp.zeros_like(state_ref)

    @pl.when(c > 0)
    def _():
        xbuf_ref[0:halo, :] = xbuf_ref[L:L + halo, :]

    xbuf_ref[halo:halo + L, :] = xbc_ref[...]
    w = cw_ref[...]
    conv = cb_ref[...] + w[3:4, :] * xbuf_ref[halo:halo + L, :]
    for kk in range(CONV_WIDTH - 1):
        conv = conv + w[kk:kk + 1, :] * xbuf_ref[pl.ds(halo - (CONV_WIDTH - 1) + kk, L), :]
    u = conv * _sigmoid(conv)
    xs = u[:, :D_SSD]
    bm = u[:, D_SSD:D_SSD + SSD_GROUPS * SSD_STATE]
    cm = u[:, D_SSD + SSD_GROUPS * SSD_STATE:]

    dt = _softplus(small_ref[...] + dtb_ref[...])
    a = dt * (-jnp.exp(alog_ref[...]))
    acs = _dot_01_x(_tril01(L), a)
    e1 = e1_ref[...]
    dt_e = _dot_x_01(dt, e1)
    acs_e = _dot_x_01(acs, e1)
    acs_t = acs.T
    last_e = acs_e[L - 1:L, :]
    x_dt = xs * dt_e
    x_dt_bf = x_dt.astype(BF16)
    x_end_bf = (x_dt * jnp.exp(last_e - acs_e)).astype(BF16)
    grow = jnp.exp(acs_e)
    chunk_decay = jnp.exp(last_e)

    row = lax.broadcasted_iota(jnp.int32, (L, L), 0)
    col = lax.broadcasted_iota(jnp.int32, (L, L), 1)
    causal = col <= row
    lo = lax.broadcasted_iota(jnp.int32, (1, LANES), 1) < SSD_HEAD_DIM
    pairs_per_group = SSD_HEADS // SSD_GROUPS // 2

    ys = []
    for g in range(SSD_GROUPS):
        bg = bm[:, g * SSD_STATE:(g + 1) * SSD_STATE]
        bg_bf = bg.astype(BF16)
        bg_t_bf = bg.T.astype(BF16)
        cg_bf = cm[:, g * SSD_STATE:(g + 1) * SSD_STATE].astype(BF16)
        cb = _dot_nt(cg_bf, bg_bf)
        for jj in range(pairs_per_group):
            j = g * pairs_per_group + jj
            sl = slice(j * LANES, (j + 1) * LANES)
            xp = x_dt_bf[:, sl]
            yd = []
            for e in range(2):
                h = 2 * j + e
                hl = HEAD_LANE0 + h
                seg = jnp.broadcast_to(acs[:, hl:hl + 1], (L, L)) - acs_t[hl:hl + 1, :]
                dec = jnp.exp(jnp.where(causal, seg, NEG_BIG))
                yd.append(_dot((cb * dec).astype(BF16), xp))
            st = state_ref[j]
            y_off = _dot(cg_bf, st.astype(BF16)) * grow[:, sl]
            ys.append(jnp.where(lo, yd[0], yd[1]) + y_off)
            state_ref[j] = st * chunk_decay[:, sl] + _dot(bg_t_bf, x_end_bf[:, sl])

    y = jnp.concatenate(ys, axis=1) + xs * dskip_ref[...]
    zz = z_ref[...]
    y = y * (zz * _sigmoid(zz))
    o_ref[...] = _rms(y, ng_ref[...]).astype(o_ref.dtype)


def _head_expand_mat():
    e1 = np.zeros((LANES, D_SSD), np.float32)
    for h in range(SSD_HEADS):
        e1[HEAD_LANE0 + h, h * SSD_HEAD_DIM:(h + 1) * SSD_HEAD_DIM] = 1.0
    return jnp.asarray(e1, BF16)


def _ssd(xbc3, z3, small3, conv_w, conv_b, dtb_row, alog_row, dskip_row, norm_g):
    B, S, _ = xbc3.shape
    L = min(SSD_CHUNK, S)
    e1 = _head_expand_mat()
    const = lambda shape: pl.BlockSpec(shape, lambda b, c: (0,) * len(shape))
    return pl.pallas_call(
        functools.partial(_ssd_kernel, L=L),
        grid=(B, S // L),
        in_specs=[
            pl.BlockSpec((None, L, CONV_CH), lambda b, c: (b, c, 0)),
            pl.BlockSpec((None, L, D_SSD), lambda b, c: (b, c, 0)),
            pl.BlockSpec((None, L, SMALL_W), lambda b, c: (b, c, 0)),
            const((CONV_WIDTH, CONV_CH)), const((1, CONV_CH)), const((1, SMALL_W)), const((1, SMALL_W)),
            const((1, D_SSD)), const((1, D_SSD)), const(e1.shape),
        ],
        out_specs=pl.BlockSpec((None, L, D_SSD), lambda b, c: (b, c, 0)),
        out_shape=jax.ShapeDtypeStruct((B, S, D_SSD), BF16),
        scratch_shapes=[pltpu.VMEM((L + 8, CONV_CH), F32),
                        pltpu.VMEM((SSD_HEADS // 2, SSD_STATE, LANES), F32)],
        compiler_params=_cparams(("parallel", "arbitrary")),
        name="conv_ssd",
    )(xbc3, z3, small3, conv_w, conv_b, dtb_row, alog_row, dskip_row, norm_g, e1)


def _merge_kernel(ya_ref, ys_ref, gates_ref, x_ref, wa_ref, ws_ref, wo_ref, g2_ref, wr_ref, br_ref,
                  xo_ref, h_ref, lg_ref):
    D = x_ref.shape[-1]
    pa = _dot(ya_ref[...], wa_ref[...])
    ps = _dot(ys_ref[...], ws_ref[...])
    merged = _sigmoid(gates_ref[:, :D]) * pa + _sigmoid(gates_ref[:, D:]) * ps
    xn = x_ref[...] + _dot(merged.astype(BF16), wo_ref[...])
    xo_ref[...] = xn
    h = _rms(xn, g2_ref[...])
    h_ref[...] = h.astype(BF16)
    h_hi, h_mid, _ = _split3(h)
    w_hi, w_mid, _ = _split3(wr_ref[...])
    lg_ref[...] = _dot(h_hi, w_hi) + _dot(h_hi, w_mid) + _dot(h_mid, w_hi) + br_ref[...]


def _merge(ya, ys, gates, x2, wa, ws, wo, g2, wr, br):
    T, D = x2.shape
    tm = min(ROW_TILE, T)
    row = lambda n: pl.BlockSpec((tm, n), lambda i: (i, 0))
    const = lambda a: pl.BlockSpec(a.shape, lambda i: (0, 0))
    return pl.pallas_call(
        _merge_kernel,
        grid=(T // tm,),
        in_specs=[row(D_ATTN), row(D_SSD), row(2 * D), row(D),
                  const(wa), const(ws), const(wo), const(g2), const(wr), const(br)],
        out_specs=[row(D), row(D), row(LANES)],
        out_shape=[jax.ShapeDtypeStruct((T, D), F32), jax.ShapeDtypeStruct((T, D), BF16),
                   jax.ShapeDtypeStruct((T, LANES), F32)],
        compiler_params=_cparams(("parallel",)),
        name="merge_outproj_router",
    )(ya, ys, gates, x2, wa, ws, wo, g2, wr, br)


def _route_kernel(lg_ref, route_ref, cnt_ref, carry_ref):
    i = pl.program_id(0)

    @pl.when(i == 0)
    def _():
        carry_ref[...] = jnp.zeros_like(carry_ref)

    lg = lg_ref[...]
    tm = lg.shape[0]
    lane_i = lax.broadcasted_iota(jnp.int32, (tm, LANES), 1)
    lane = lane_i.astype(F32)
    first = lambda cond: jnp.min(jnp.where(cond, lane, float(LANES)), axis=-1, keepdims=True)

    gmask = lane_i < N_EXPERT_GROUPS
    gl = jnp.where(gmask, lg, NEG_BIG)
    gmax = jnp.max(gl, axis=-1, keepdims=True)
    gsum = jnp.sum(jnp.where(gmask, jnp.exp(gl - gmax), 0.0), axis=-1, keepdims=True)
    g_idx = first(gmask & (gl == gmax))
    g_w = 1.0 / gsum

    e_lo = N_EXPERT_GROUPS + g_idx * EXPERTS_PER_GROUP
    emask = (lane >= e_lo) & (lane < e_lo + EXPERTS_PER_GROUP)
    el = jnp.where(emask, lg, NEG_BIG)
    emax = jnp.max(el, axis=-1, keepdims=True)
    esum = jnp.sum(jnp.where(emask, jnp.exp(el - emax), 0.0), axis=-1, keepdims=True)
    l0 = first(emask & (el == emax))
    el2 = jnp.where(lane == l0, NEG_BIG, el)
    m2 = jnp.max(el2, axis=-1, keepdims=True)
    l1 = first(emask & (lane != l0) & (el2 == m2))
    gate0 = g_w * (1.0 / esum)
    gate1 = g_w * (jnp.exp(m2 - emax) / esum)

    is0 = lane == l0
    is1 = lane == l1
    onehot = jnp.where(is0 | is1, 1.0, 0.0)
    before = _dot(_tril01(tm, strict=True), onehot.astype(BF16)) + carry_ref[0:1, :]
    rank0 = jnp.sum(jnp.where(is0, before, 0.0), axis=-1, keepdims=True)
    rank1 = jnp.sum(jnp.where(is1, before, 0.0), axis=-1, keepdims=True)
    total = carry_ref[0:1, :] + jnp.sum(onehot, axis=0, keepdims=True)
    carry_ref[0:1, :] = total
    cnt_ref[...] = jnp.broadcast_to(total, cnt_ref.shape)

    rec = jnp.zeros((tm, LANES), F32)
    for ln, val in ((R_E0, l0 - N_EXPERT_GROUPS), (R_E1, l1 - N_EXPERT_GROUPS), (R_G0, gate0), (R_G1, gate1),
                    (R_RANK0, rank0), (R_RANK1, rank1)):
        rec = jnp.where(lane_i == ln, val, rec)
    route_ref[...] = rec


def _route(logits):
    T = logits.shape[0]
    tm = min(ROW_TILE, T)
    return pl.pallas_call(
        _route_kernel,
        grid=(T // tm,),
        in_specs=[pl.BlockSpec((tm, LANES), lambda i: (i, 0))],
        out_specs=[pl.BlockSpec((tm, LANES), lambda i: (i, 0)), pl.BlockSpec((8, LANES), lambda i: (0, 0))],
        out_shape=[jax.ShapeDtypeStruct((T, LANES), F32), jax.ShapeDtypeStruct((8, LANES), F32)],
        scratch_shapes=[pltpu.VMEM((8, LANES), F32)],
        compiler_params=_cparams(("arbitrary",)),
        name="route",
    )(logits)


def _table_lookup(idx, table):
    n = table.shape[0]
    return jnp.sum(jnp.where(idx[:, None] == jnp.arange(n, dtype=idx.dtype)[None, :], table[None, :], 0), axis=1)


def _dispatch_plan(route, cnt):
    T = route.shape[0]
    counts = cnt[0, N_EXPERT_GROUPS:N_EXPERT_GROUPS + N_EXPERTS].astype(jnp.int32)
    padded = (counts + MOE_BLOCK - 1) // MOE_BLOCK * MOE_BLOCK
    pad_end = jnp.cumsum(padded)
    pad_start = pad_end - padded
    start = jnp.cumsum(counts) - counts
    e0 = route[:, R_E0].astype(jnp.int32)
    e1 = route[:, R_E1].astype(jnp.int32)
    dest0 = _table_lookup(e0, pad_start) + route[:, R_RANK0].astype(jnp.int32)
    dest1 = _table_lookup(e1, pad_start) + route[:, R_RANK1].astype(jnp.int32)
    n_blocks = -(-(T * TOP_K) // MOE_BLOCK) + N_EXPERTS
    block_end = pad_end // MOE_BLOCK
    blk = jnp.arange(n_blocks, dtype=jnp.int32)
    block_expert = jnp.minimum(jnp.sum(blk[:, None] >= block_end[None, :], axis=1), N_EXPERTS - 1).astype(jnp.int32)
    n_used = block_end[-1:].astype(jnp.int32)
    order = jnp.argsort(jnp.stack([e0, e1], axis=1).reshape(-1), stable=True).astype(jnp.int32)
    shift = _table_lookup(block_expert, start - pad_start)
    pos = jnp.arange(n_blocks * MOE_BLOCK, dtype=jnp.int32) + jnp.repeat(shift, MOE_BLOCK)
    src_tok = order[jnp.clip(pos, 0, T * TOP_K - 1)] // TOP_K
    return dest0, dest1, src_tok, block_expert, n_used


def _expert_kernel(be_ref, nu_ref, x_ref, w1_ref, w3_ref, w2_ref, o_ref, w1b_ref, w3b_ref, w2b_ref):
    i = pl.program_id(0)
    last = nu_ref[0] - 1
    cur = be_ref[jnp.minimum(i, last)]
    prev = be_ref[jnp.minimum(jnp.maximum(i - 1, 0), last)]

    @pl.when((i == 0) | (cur != prev))
    def _():
        w1b_ref[...] = w1_ref[...].astype(BF16)
        w3b_ref[...] = w3_ref[...].astype(BF16)
        w2b_ref[...] = w2_ref[...].astype(BF16)

    @pl.when(i <= last)
    def _():
        x = x_ref[...]
        h1 = _dot(x, w1b_ref[...])
        h3 = _dot(x, w3b_ref[...])
        a = (h1 * _sigmoid(h1) * h3).astype(BF16)
        o_ref[...] = _dot(a, w2b_ref[...])


def _experts(buf, block_expert, n_used, w1, w3, w2, layer):
    R, D = buf.shape
    FF = w1.shape[-1]
    nb = R // MOE_BLOCK
    blk = lambda i, be, nu: (jnp.minimum(i, nu[0] - 1), 0)
    wsel = lambda i, be, nu: (layer, be[jnp.minimum(i, nu[0] - 1)], 0, 0)
    return pl.pallas_call(
        _expert_kernel,
        grid_spec=pltpu.PrefetchScalarGridSpec(
            num_scalar_prefetch=2,
            grid=(nb,),
            in_specs=[
                pl.BlockSpec((MOE_BLOCK, D), blk),
                pl.BlockSpec((None, None, D, FF), wsel),
                pl.BlockSpec((None, None, D, FF), wsel),
                pl.BlockSpec((None, None, FF, D), wsel),
            ],
            out_specs=pl.BlockSpec((MOE_BLOCK, D), blk),
            scratch_shapes=[pltpu.VMEM((D, FF), BF16), pltpu.VMEM((D, FF), BF16), pltpu.VMEM((FF, D), BF16)],
        ),
        out_shape=jax.ShapeDtypeStruct((R, D), F32),
        compiler_params=_cparams(("arbitrary",)),
        name="expert_mlp",
    )(block_expert, n_used, buf, w1, w3, w2)


def _final_kernel(x_ref, r0_ref, r1_ref, rt_ref, g_ref, o_ref):
    o_ref[...] = _rms(_moe_combine(x_ref, r0_ref, r1_ref, rt_ref), g_ref[...])


def _final_norm(x2, moe, g):
    T, D = x2.shape
    tm = min(ROW_TILE, T)
    row = lambda n: pl.BlockSpec((tm, n), lambda i: (i, 0))
    return pl.pallas_call(
        _final_kernel,
        grid=(T // tm,),
        in_specs=[row(D)] + _moe_specs(T, tm, D) + [pl.BlockSpec((1, D), lambda i: (0, 0))],
        out_specs=row(D),
        out_shape=jax.ShapeDtypeStruct((T, D), F32),
        compiler_params=_cparams(("parallel",)),
        name="final_norm",
    )(x2, moe[0], moe[0], moe[1], g)


def _pack_plan(D):
    widths = (D_ATTN, D_ATTN, D_ATTN, ATTN_HEADS, D_SSD, CONV_CH, SSD_HEADS, D, D)
    src = dict(zip(("q", "k", "v", "f", "z", "xbc", "dt", "ga", "gs"),
                   zip(np.concatenate([[0], np.cumsum(widths)[:-1]]).tolist(), widths)))
    plan, dst = [], 0
    for name in ("q", "k", "v", "z", "xbc", "ga", "gs", "f", "dt"):
        s0, w = src[name]
        plan.append((s0, w, dst, ATTN_HEAD_DIM ** -0.5 * LOG2E if name == "q" else 1.0))
        dst += w
    return plan, dst


def _pack_kernel(w_ref, o_ref, *, plan, used):
    for s0, w, d0, scale in plan:
        piece = w_ref[:, s0:s0 + w]
        if scale != 1.0:
            piece = piece * scale
        o_ref[:, d0:d0 + w] = piece.astype(BF16)
    o_ref[:, used:] = jnp.zeros((o_ref.shape[0], o_ref.shape[1] - used), BF16)


def _pack_w_in(w_in):
    depth, D, cols = w_in.shape
    plan, used = _pack_plan(D)
    n_out = -(-used // LANES) * LANES
    rows = 128
    return pl.pallas_call(
        functools.partial(_pack_kernel, plan=plan, used=used),
        grid=(depth, D // rows),
        in_specs=[pl.BlockSpec((None, rows, cols), lambda l, r: (l, r, 0))],
        out_specs=pl.BlockSpec((None, rows, n_out), lambda l, r: (l, r, 0)),
        out_shape=jax.ShapeDtypeStruct((depth, D, n_out), BF16),
        compiler_params=_cparams(("parallel", "parallel")),
        name="pack_w_in",
    )(w_in)


def _small_rows(mat, lane0):
    depth, n = mat.shape
    return jnp.zeros((depth, 1, SMALL_W), F32).at[:, 0, lane0:lane0 + n].set(mat.astype(F32))


def kernel(x, norm_mix_g, w_in, b_f, conv_w, conv_b, dt_bias, a_log, d_skip, ssd_norm_g, w_br_attn, w_br_ssd,
           w_out, norm_ffn_g, w_group_router, b_group_router, w_expert_router, b_expert_router, w1, w3, w2,
           final_g):
    B, S, D = x.shape
    T = B * S
    depth = w_in.shape[0]
    w_packed = _pack_w_in(w_in)
    bf_rows = _small_rows(b_f, 0)
    dtb_rows = _small_rows(dt_bias, HEAD_LANE0)
    alog_rows = _small_rows(a_log, HEAD_LANE0)
    dskip_rows = jnp.repeat(d_skip, SSD_HEAD_DIM, axis=1)[:, None, :]
    router_pad = LANES - N_EXPERT_GROUPS - N_EXPERTS
    w_router = jnp.concatenate([w_group_router, w_expert_router, jnp.zeros((depth, D, router_pad), F32)], axis=2)
    b_router = jnp.concatenate([b_group_router, b_expert_router, jnp.zeros((depth, router_pad), F32)],
                               axis=1)[:, None, :]
    wa, ws, wo = w_br_attn.astype(BF16), w_br_ssd.astype(BF16), w_out.astype(BF16)

    def trunk(xb):
        Bb = xb.shape[0]
        Tb = Bb * S
        x2 = xb.reshape(Tb, D)
        moe = None
        for l in range(depth):
            x2, (qkv, z, xbc, gates, small) = _inproj(x2, moe, norm_mix_g[l][None, :], w_packed, l)
            small3 = small.reshape(Bb, S, SMALL_W)
            fk = _fcum(small3, bf_rows[l])
            y_attn = _attention(qkv.reshape(Bb, S, 3 * D_ATTN), fk).reshape(Tb, D_ATTN)
            y_ssd = _ssd(xbc.reshape(Bb, S, CONV_CH), z.reshape(Bb, S, D_SSD), small3, conv_w[l],
                         conv_b[l][None, :], dtb_rows[l], alog_rows[l], dskip_rows[l],
                         ssd_norm_g[l][None, :]).reshape(Tb, D_SSD)
            x2, h2, logits = _merge(y_attn, y_ssd, gates, x2, wa[l], ws[l], wo[l], norm_ffn_g[l][None, :],
                                    w_router[l], b_router[l])
            route, cnt = _route(logits)
            dest0, dest1, src_tok, block_expert, n_used = _dispatch_plan(route, cnt)
            out = _experts(h2[src_tok], block_expert, n_used, w1, w3, w2, l)
            moe = (out[jnp.concatenate([dest0, dest1])], route)
        return _final_norm(x2, moe, final_g[None, :]).reshape(Bb, S, D)

    return trunk(x)
```

```python
import functools

import numpy as np
import jax
import jax.numpy as jnp
from jax import lax
from jax.experimental import pallas as pl
from jax.experimental.pallas import tpu as pltpu

F32 = jnp.float32
BF16 = jnp.bfloat16

ATTN_HEADS = 8
ATTN_HEAD_DIM = 64
D_ATTN = ATTN_HEADS * ATTN_HEAD_DIM
SSD_HEADS = 16
SSD_HEAD_DIM = 64
D_SSD = SSD_HEADS * SSD_HEAD_DIM
SSD_GROUPS = 2
SSD_STATE = 128
CONV_WIDTH = 4
CONV_CH = D_SSD + 2 * SSD_GROUPS * SSD_STATE
N_EXPERT_GROUPS = 4
EXPERTS_PER_GROUP = 8
N_EXPERTS = N_EXPERT_GROUPS * EXPERTS_PER_GROUP
TOP_K = 2
EPS = 1e-6

LANES = 128
HEAD_LANE0 = 8
SMALL_W = LANES
NEG_BIG = -1e30
LOG2E = 1.4426950408889634
VMEM_LIMIT = 56 * 1024 * 1024

SSD_CHUNK = 128
ATTN_TILE = 1024
FCUM_TILE = 512
ROW_TILE = 512
MOE_BLOCK = 256
R_E0, R_E1, R_G0, R_G1, R_RANK0, R_RANK1 = range(6)


def _cparams(sem):
    return pltpu.CompilerParams(dimension_semantics=sem, vmem_limit_bytes=VMEM_LIMIT)


def _split3(x):
    hi = x.astype(BF16)
    r = x - hi.astype(F32)
    mid = r.astype(BF16)
    lo = (r - mid.astype(F32)).astype(BF16)
    return hi, mid, lo


def _dot(a, b):
    return jnp.dot(a, b, preferred_element_type=F32)


def _dot_nt(a, b):
    return lax.dot_general(a, b, (((1,), (1,)), ((), ())), preferred_element_type=F32)


def _dot_x_01(x, m01):
    hi, mid, lo = _split3(x)
    return _dot(hi, m01) + _dot(mid, m01) + _dot(lo, m01)


def _dot_01_x(m01, x):
    hi, mid, lo = _split3(x)
    return _dot(m01, hi) + _dot(m01, mid) + _dot(m01, lo)


def _tril01(n, strict=False):
    r = lax.broadcasted_iota(jnp.int32, (n, n), 0)
    c = lax.broadcasted_iota(jnp.int32, (n, n), 1)
    return ((c < r) if strict else (c <= r)).astype(BF16)


def _softplus(x):
    return jnp.maximum(x, 0.0) + jnp.log1p(jnp.exp(-jnp.abs(x)))


def _sigmoid(x):
    return 0.5 * jnp.tanh(0.5 * x) + 0.5


def _rms(x, g):
    ms = jnp.mean(x * x, axis=-1, keepdims=True)
    return x * lax.rsqrt(ms + EPS) * g


def _moe_combine(x_ref, r0_ref, r1_ref, rt_ref):
    rt = rt_ref[...]
    D = x_ref.shape[-1]
    rows = lambda ref: ref[:, :D].astype(F32) + ref[:, D:].astype(F32)
    return x_ref[...] + (rt[:, R_G0:R_G0 + 1] * rows(r0_ref) + rt[:, R_G1:R_G1 + 1] * rows(r1_ref))


def _inproj_kernel(*refs, with_moe):
    if with_moe:
        x_ref, r0_ref, r1_ref, rt_ref, g_ref, w_ref, xo_ref, *outs = refs
        x = _moe_combine(x_ref, r0_ref, r1_ref, rt_ref)
        xo_ref[...] = x
    else:
        x_ref, g_ref, w_ref, *outs = refs
        x = x_ref[...]
    h = _rms(x, g_ref[...]).astype(BF16)
    off = 0
    for ref in outs:
        n = ref.shape[-1]
        for c0 in range(0, n, 512):
            c1 = min(c0 + 512, n)
            ref[:, c0:c1] = _dot(h, w_ref[:, off + c0:off + c1]).astype(ref.dtype)
        off += n


def _inproj(x2, moe, g, w_packed, layer):
    T, D = x2.shape
    widths = (3 * D_ATTN, D_SSD, CONV_CH, 2 * D, SMALL_W)
    dtypes = (BF16, F32, F32, F32, F32)
    tm = min(ROW_TILE, T)
    row = lambda n: pl.BlockSpec((tm, n), lambda i: (i, 0))
    with_moe = moe is not None
    ins = [x2] + (list(moe) if with_moe else []) + [g, w_packed]
    in_specs = [row(D)] + ([row(2 * D), row(2 * D), row(LANES)] if with_moe else []) + [
        pl.BlockSpec((1, D), lambda i: (0, 0)),
        pl.BlockSpec((None,) + w_packed.shape[1:], lambda i: (layer, 0, 0), pipeline_mode=pl.Buffered(1)),
    ]
    out_specs = ([row(D)] if with_moe else []) + [row(n) for n in widths]
    out_shape = ([jax.ShapeDtypeStruct((T, D), F32)] if with_moe else []) + [
        jax.ShapeDtypeStruct((T, n), dt) for n, dt in zip(widths, dtypes)]
    res = pl.pallas_call(
        functools.partial(_inproj_kernel, with_moe=with_moe),
        grid=(T // tm,),
        in_specs=in_specs,
        out_specs=out_specs,
        out_shape=out_shape,
        compiler_params=_cparams(("parallel",)),
        name="inproj",
    )(*ins)
    return (res[0], res[1:]) if with_moe else (x2, res)


N_FPARTS = 3
HEAD_PAIRS = D_ATTN // LANES


def _fcum_kernel(s_ref, b_ref, place_ref, fk_ref, carry_ref):
    c = pl.program_id(1)

    @pl.when(c == 0)
    def _():
        carry_ref[...] = jnp.zeros_like(carry_ref)

    x = s_ref[...] + b_ref[...]
    logf = jnp.minimum(x, 0.0) - jnp.log1p(jnp.exp(-jnp.abs(x)))
    n = x.shape[0]
    cum = _dot_01_x(_tril01(n), logf) + carry_ref[0:1, :]
    carry_ref[0:1, :] = cum[n - 1:n, :]
    parts = jnp.concatenate(_split3(cum * (-LOG2E)), axis=1)
    for p in range(HEAD_PAIRS):
        fk_ref[p] = _dot(parts, place_ref[p]).astype(BF16)


def _fpart_placement():
    m = np.zeros((HEAD_PAIRS, N_FPARTS * LANES, LANES), np.float32)
    for p in range(HEAD_PAIRS):
        for j in range(N_FPARTS):
            m[p, j * LANES + 2 * p, ATTN_HEAD_DIM + j] = 1.0
            m[p, j * LANES + 2 * p + 1, j] = 1.0
    return jnp.asarray(m, BF16)


def _fcum(small3, bf_row):
    B, S, _ = small3.shape
    L = min(FCUM_TILE, S)
    place = _fpart_placement()
    return pl.pallas_call(
        _fcum_kernel,
        grid=(B, S // L),
        in_specs=[
            pl.BlockSpec((None, L, SMALL_W), lambda b, c: (b, c, 0)),
            pl.BlockSpec((1, SMALL_W), lambda b, c: (0, 0)),
            pl.BlockSpec(place.shape, lambda b, c: (0, 0, 0)),
        ],
        out_specs=pl.BlockSpec((None, HEAD_PAIRS, L, LANES), lambda b, c: (b, 0, c, 0)),
        out_shape=jax.ShapeDtypeStruct((B, HEAD_PAIRS, S, LANES), BF16),
        scratch_shapes=[pltpu.VMEM((8, SMALL_W), F32)],
        compiler_params=_cparams(("parallel", "arbitrary")),
        name="forget_cumsum",
    )(small3, bf_row, place)


def _attn_kernel(q_ref, k_ref, v_ref, fk_ref, o_ref, ka_ref, kb_ref, va_ref, vb_ref, s_ref, p_ref, m_ref,
                 acc_ref, *, t):
    i = pl.program_id(2)
    lane = lax.broadcasted_iota(jnp.int32, (1, LANES), 1)
    lo = lane < ATTN_HEAD_DIM
    den_lane = (ATTN_HEAD_DIM, 0)
    ones_at = lambda cond: jnp.where(cond, 1.0, 0.0).astype(BF16)

    @pl.when(i == 0)
    def _():
        k = k_ref[...]
        fk = fk_ref[...]
        ka_ref[...] = jnp.where(lo, k, fk)
        kb_ref[...] = jnp.where(lo, fk, k)
        v = v_ref[...]
        va_ref[...] = jnp.where(lo, v, ones_at(lane == den_lane[0]))
        vb_ref[...] = jnp.where(lo, ones_at(lane == den_lane[1]), v)

    q = q_ref[...]
    bias_a = ones_at((lane >= ATTN_HEAD_DIM) & (lane < ATTN_HEAD_DIM + N_FPARTS))
    bias_b = ones_at(lane < N_FPARTS)
    q_heads = (jnp.where(lo, q, bias_a), jnp.where(lo, bias_b, q))
    k_refs = (ka_ref, kb_ref)
    v_refs = (va_ref, vb_ref)
    m_ref[...] = jnp.full(m_ref.shape, NEG_BIG, F32)
    acc_ref[...] = jnp.zeros(acc_ref.shape, F32)

    def step(kb, r0, nr, nc, masked):
        ks = pl.multiple_of(kb * t, t)
        rows = slice(r0, r0 + nr)
        for e in range(2):
            s = _dot_nt(q_heads[e][rows], k_refs[e][pl.ds(ks, nc), :])
            if masked:
                r = r0 + lax.broadcasted_iota(jnp.int32, (nr, nc), 0)
                c = lax.broadcasted_iota(jnp.int32, (nr, nc), 1)
                s = jnp.where(c <= r, s, NEG_BIG)
            s_ref[e, rows, 0:nc] = s
        for e in range(2):
            m_prev = m_ref[e, rows, :]
            m_new = jnp.maximum(m_prev, jnp.max(s_ref[e, rows, 0:nc], axis=-1, keepdims=True))
            m_ref[e, rows, :] = m_new
            p_ref[e, rows, 0:nc] = jnp.exp2(
                s_ref[e, rows, 0:nc] - jnp.concatenate([m_new] * (nc // LANES), axis=1)).astype(BF16)
            pv = _dot(p_ref[e, rows, 0:nc], v_refs[e][pl.ds(ks, nc), :])
            acc_ref[e, rows, :] = acc_ref[e, rows, :] * jnp.exp2(m_prev - m_new) + pv

    def body(kb, carry):
        step(kb, 0, t, t, False)
        return carry

    lax.fori_loop(0, i, body, 0)
    half = t // 2
    step(i, 0, half, half, True)
    step(i, half, half, t, True)
    acc_a = acc_ref[0]
    acc_b = acc_ref[1]
    inv_a = 1.0 / acc_a[:, den_lane[0]:den_lane[0] + 1]
    inv_b = 1.0 / acc_b[:, den_lane[1]:den_lane[1] + 1]
    o_ref[...] = jnp.where(lo, acc_a * inv_a, acc_b * inv_b).astype(o_ref.dtype)


def _attention(qkv3, fk):
    B, S, _ = qkv3.shape
    t = min(ATTN_TILE, S)
    seq = lambda: pltpu.VMEM((S, LANES), BF16)
    return pl.pallas_call(
        functools.partial(_attn_kernel, t=t),
        grid=(B, HEAD_PAIRS, S // t),
        in_specs=[
            pl.BlockSpec((None, t, LANES), lambda b, h, i: (b, i, h)),
            pl.BlockSpec((None, S, LANES), lambda b, h, i: (b, 0, HEAD_PAIRS + h)),
            pl.BlockSpec((None, S, LANES), lambda b, h, i: (b, 0, 2 * HEAD_PAIRS + h)),
            pl.BlockSpec((None, None, S, LANES), lambda b, h, i: (b, h, 0, 0)),
        ],
        out_specs=pl.BlockSpec((None, t, LANES), lambda b, h, i: (b, i, h)),
        out_shape=jax.ShapeDtypeStruct((B, S, D_ATTN), BF16),
        scratch_shapes=[
            seq(), seq(), seq(), seq(),
            pltpu.VMEM((2, t, t), F32), pltpu.VMEM((2, t, t), BF16),
            pltpu.VMEM((2, t, LANES), F32), pltpu.VMEM((2, t, LANES), F32),
        ],
        compiler_params=_cparams(("parallel", "parallel", "arbitrary")),
        name="fox_attention",
    )(qkv3, qkv3, qkv3, fk)


def _ssd_kernel(xbc_ref, z_ref, small_ref, cw_ref, cb_ref, dtb_ref, alog_ref, dskip_ref, ng_ref,
                e1_ref, o_ref, xbuf_ref, state_ref, *, L):
    c = pl.program_id(1)
    halo = 8

    @pl.when(c == 0)
    def _():
        xbuf_ref[0:halo, :] = jnp.zeros((halo, CONV_CH), F32)
        state_ref[...] = jnp.zeros_like(state_ref)

    @pl.when(c > 0)
    def _():
        xbuf_ref[0:halo, :] = xbuf_ref[L:L + halo, :]

    xbuf_ref[halo:halo + L, :] = xbc_ref[...]
    w = cw_ref[...]
    conv = cb_ref[...] + w[3:4, :] * xbuf_ref[halo:halo + L, :]
    for kk in range(CONV_WIDTH - 1):
        conv = conv + w[kk:kk + 1, :] * xbuf_ref[pl.ds(halo - (CONV_WIDTH - 1) + kk, L), :]
    u = conv * _sigmoid(conv)
    xs = u[:, :D_SSD]
    bm = u[:, D_SSD:D_SSD + SSD_GROUPS * SSD_STATE]
    cm = u[:, D_SSD + SSD_GROUPS * SSD_STATE:]

    dt = _softplus(small_ref[...] + dtb_ref[...])
    a = dt * (-jnp.exp(alog_ref[...]))
    acs = _dot_01_x(_tril01(L), a)
    e1 = e1_ref[...]
    dt_e = _dot_x_01(dt, e1)
    acs_e = _dot_x_01(acs, e1)
    acs_t = acs.T
    last_e = acs_e[L - 1:L, :]
    x_dt = xs * dt_e
    x_dt_bf = x_dt.astype(BF16)
    x_end_bf = (x_dt * jnp.exp(last_e - acs_e)).astype(BF16)
    grow = jnp.exp(acs_e)
    chunk_decay = jnp.exp(last_e)

    row = lax.broadcasted_iota(jnp.int32, (L, L), 0)
    col = lax.broadcasted_iota(jnp.int32, (L, L), 1)
    causal = col <= row
    lo = lax.broadcasted_iota(jnp.int32, (1, LANES), 1) < SSD_HEAD_DIM
    pairs_per_group = SSD_HEADS // SSD_GROUPS // 2

    ys = []
    for g in range(SSD_GROUPS):
        bg = bm[:, g * SSD_STATE:(g + 1) * SSD_STATE]
        bg_bf = bg.astype(BF16)
        bg_t_bf = bg.T.astype(BF16)
        cg_bf = cm[:, g * SSD_STATE:(g + 1) * SSD_STATE].astype(BF16)
        cb = _dot_nt(cg_bf, bg_bf)
        for jj in range(pairs_per_group):
            j = g * pairs_per_group + jj
            sl = slice(j * LANES, (j + 1) * LANES)
            xp = x_dt_bf[:, sl]
            yd = []
            for e in range(2):
                h = 2 * j + e
                hl = HEAD_LANE0 + h
                seg = jnp.broadcast_to(acs[:, hl:hl + 1], (L, L)) - acs_t[hl:hl + 1, :]
                dec = jnp.exp(jnp.where(causal, seg, NEG_BIG))
                yd.append(_dot((cb * dec).astype(BF16), xp))
            st = state_ref[j]
            y_off = _dot(cg_bf, st.astype(BF16)) * grow[:, sl]
            ys.append(jnp.where(lo, yd[0], yd[1]) + y_off)
            state_ref[j] = st * chunk_decay[:, sl] + _dot(bg_t_bf, x_end_bf[:, sl])

    y = jnp.concatenate(ys, axis=1) + xs * dskip_ref[...]
    zz = z_ref[...]
    y = y * (zz * _sigmoid(zz))
    o_ref[...] = _rms(y, ng_ref[...]).astype(o_ref.dtype)


def _head_expand_mat():
    e1 = np.zeros((LANES, D_SSD), np.float32)
    for h in range(SSD_HEADS):
        e1[HEAD_LANE0 + h, h * SSD_HEAD_DIM:(h + 1) * SSD_HEAD_DIM] = 1.0
    return jnp.asarray(e1, BF16)


def _ssd(xbc3, z3, small3, conv_w, conv_b, dtb_row, alog_row, dskip_row, norm_g):
    B, S, _ = xbc3.shape
    L = min(SSD_CHUNK, S)
    e1 = _head_expand_mat()
    const = lambda shape: pl.BlockSpec(shape, lambda b, c: (0,) * len(shape))
    return pl.pallas_call(
        functools.partial(_ssd_kernel, L=L),
        grid=(B, S // L),
        in_specs=[
            pl.BlockSpec((None, L, CONV_CH), lambda b, c: (b, c, 0)),
            pl.BlockSpec((None, L, D_SSD), lambda b, c: (b, c, 0)),
            pl.BlockSpec((None, L, SMALL_W), lambda b, c: (b, c, 0)),
            const((CONV_WIDTH, CONV_CH)), const((1, CONV_CH)), const((1, SMALL_W)), const((1, SMALL_W)),
            const((1, D_SSD)), const((1, D_SSD)), const(e1.shape),
        ],
        out_specs=pl.BlockSpec((None, L, D_SSD), lambda b, c: (b, c, 0)),
        out_shape=jax.ShapeDtypeStruct((B, S, D_SSD), BF16),
        scratch_shapes=[pltpu.VMEM((L + 8, CONV_CH), F32),
                        pltpu.VMEM((SSD_HEADS // 2, SSD_STATE, LANES), F32)],
        compiler_params=_cparams(("parallel", "arbitrary")),
        name="conv_ssd",
    )(xbc3, z3, small3, conv_w, conv_b, dtb_row, alog_row, dskip_row, norm_g, e1)


def _merge_kernel(ya_ref, ys_ref, gates_ref, x_ref, wa_ref, ws_ref, wo_ref, g2_ref, wr_ref, br_ref,
                  xo_ref, h_ref, lg_ref):
    D = x_ref.shape[-1]
    pa = _dot(ya_ref[...], wa_ref[...])
    ps = _dot(ys_ref[...], ws_ref[...])
    merged = _sigmoid(gates_ref[:, :D]) * pa + _sigmoid(gates_ref[:, D:]) * ps
    xn = x_ref[...] + _dot(merged.astype(BF16), wo_ref[...])
    xo_ref[...] = xn
    h = _rms(xn, g2_ref[...])
    h_ref[...] = h.astype(BF16)
    h_hi, h_mid, _ = _split3(h)
    w_hi, w_mid, _ = _split3(wr_ref[...])
    lg_ref[...] = _dot(h_hi, w_hi) + _dot(h_hi, w_mid) + _dot(h_mid, w_hi) + br_ref[...]


def _merge(ya, ys, gates, x2, wa, ws, wo, g2, wr, br):
    T, D = x2.shape
    tm = min(ROW_TILE, T)
    row = lambda n: pl.BlockSpec((tm, n), lambda i: (i, 0))
    const = lambda a: pl.BlockSpec(a.shape, lambda i: (0, 0))
    return pl.pallas_call(
        _merge_kernel,
        grid=(T // tm,),
        in_specs=[row(D_ATTN), row(D_SSD), row(2 * D), row(D),
                  const(wa), const(ws), const(wo), const(g2), const(wr), const(br)],
        out_specs=[row(D), row(D), row(LANES)],
        out_shape=[jax.ShapeDtypeStruct((T, D), F32), jax.ShapeDtypeStruct((T, D), BF16),
                   jax.ShapeDtypeStruct((T, LANES), F32)],
        compiler_params=_cparams(("parallel",)),
        name="merge_outproj_router",
    )(ya, ys, gates, x2, wa, ws, wo, g2, wr, br)


def _route_kernel(lg_ref, route_ref, cnt_ref, carry_ref):
    i = pl.program_id(0)

    @pl.when(i == 0)
    def _():
        carry_ref[...] = jnp.zeros_like(carry_ref)

    lg = lg_ref[...]
    tm = lg.shape[0]
    lane_i = lax.broadcasted_iota(jnp.int32, (tm, LANES), 1)
    lane = lane_i.astype(F32)
    first = lambda cond: jnp.min(jnp.where(cond, lane, float(LANES)), axis=-1, keepdims=True)

    gmask = lane_i < N_EXPERT_GROUPS
    gl = jnp.where(gmask, lg, NEG_BIG)
    gmax = jnp.max(gl, axis=-1, keepdims=True)
    gsum = jnp.sum(jnp.where(gmask, jnp.exp(gl - gmax), 0.0), axis=-1, keepdims=True)
    g_idx = first(gmask & (gl == gmax))
    g_w = 1.0 / gsum

    e_lo = N_EXPERT_GROUPS + g_idx * EXPERTS_PER_GROUP
    emask = (lane >= e_lo) & (lane < e_lo + EXPERTS_PER_GROUP)
    el = jnp.where(emask, lg, NEG_BIG)
    emax = jnp.max(el, axis=-1, keepdims=True)
    esum = jnp.sum(jnp.where(emask, jnp.exp(el - emax), 0.0), axis=-1, keepdims=True)
    l0 = first(emask & (el == emax))
    el2 = jnp.where(lane == l0, NEG_BIG, el)
    m2 = jnp.max(el2, axis=-1, keepdims=True)
    l1 = first(emask & (lane != l0) & (el2 == m2))
    gate0 = g_w * (1.0 / esum)
    gate1 = g_w * (jnp.exp(m2 - emax) / esum)

    is0 = lane == l0
    is1 = lane == l1
    onehot = jnp.where(is0 | is1, 1.0, 0.0)
    before = _dot(_tril01(tm, strict=True), onehot.astype(BF16)) + carry_ref[0:1, :]
    rank0 = jnp.sum(jnp.where(is0, before, 0.0), axis=-1, keepdims=True)
    rank1 = jnp.sum(jnp.where(is1, before, 0.0), axis=-1, keepdims=True)
    total = carry_ref[0:1, :] + jnp.sum(onehot, axis=0, keepdims=True)
    carry_ref[0:1, :] = total
    cnt_ref[...] = jnp.broadcast_to(total, cnt_ref.shape)

    rec = jnp.zeros((tm, LANES), F32)
    for ln, val in ((R_E0, l0 - N_EXPERT_GROUPS), (R_E1, l1 - N_EXPERT_GROUPS), (R_G0, gate0), (R_G1, gate1),
                    (R_RANK0, rank0), (R_RANK1, rank1)):
        rec = jnp.where(lane_i == ln, val, rec)
    route_ref[...] = rec


def _route(logits):
    T = logits.shape[0]
    tm = min(ROW_TILE, T)
    return pl.pallas_call(
        _route_kernel,
        grid=(T // tm,),
        in_specs=[pl.BlockSpec((tm, LANES), lambda i: (i, 0))],
        out_specs=[pl.BlockSpec((tm, LANES), lambda i: (i, 0)), pl.BlockSpec((8, LANES), lambda i: (0, 0))],
        out_shape=[jax.ShapeDtypeStruct((T, LANES), F32), jax.ShapeDtypeStruct((8, LANES), F32)],
        scratch_shapes=[pltpu.VMEM((8, LANES), F32)],
        compiler_params=_cparams(("arbitrary",)),
        name="route",
    )(logits)


def _table_lookup(idx, table):
    n = table.shape[0]
    return jnp.sum(jnp.where(idx[:, None] == jnp.arange(n, dtype=idx.dtype)[None, :], table[None, :], 0), axis=1)


def _dispatch_plan(route, cnt):
    T = route.shape[0]
    counts = cnt[0, N_EXPERT_GROUPS:N_EXPERT_GROUPS + N_EXPERTS].astype(jnp.int32)
    padded = (counts + MOE_BLOCK - 1) // MOE_BLOCK * MOE_BLOCK
    pad_end = jnp.cumsum(padded)
    pad_start = pad_end - padded
    start = jnp.cumsum(counts) - counts
    e0 = route[:, R_E0].astype(jnp.int32)
    e1 = route[:, R_E1].astype(jnp.int32)
    dest0 = _table_lookup(e0, pad_start) + route[:, R_RANK0].astype(jnp.int32)
    dest1 = _table_lookup(e1, pad_start) + route[:, R_RANK1].astype(jnp.int32)
    n_blocks = -(-(T * TOP_K) // MOE_BLOCK) + N_EXPERTS
    block_end = pad_end // MOE_BLOCK
    blk = jnp.arange(n_blocks, dtype=jnp.int32)
    block_expert = jnp.minimum(jnp.sum(blk[:, None] >= block_end[None, :], axis=1), N_EXPERTS - 1).astype(jnp.int32)
    n_used = block_end[-1:].astype(jnp.int32)
    order = jnp.argsort(jnp.stack([e0, e1], axis=1).reshape(-1), stable=True).astype(jnp.int32)
    shift = _table_lookup(block_expert, start - pad_start)
    pos = jnp.arange(n_blocks * MOE_BLOCK, dtype=jnp.int32) + jnp.repeat(shift, MOE_BLOCK)
    src_tok = order[jnp.clip(pos, 0, T * TOP_K - 1)] // TOP_K
    return dest0, dest1, src_tok, block_expert, n_used


def _expert_kernel(be_ref, nu_ref, x_ref, w1_ref, w3_ref, w2_ref, o_ref, w1b_ref, w3b_ref, w2b_ref):
    i = pl.program_id(0)
    last = nu_ref[0] - 1
    cur = be_ref[jnp.minimum(i, last)]
    prev = be_ref[jnp.minimum(jnp.maximum(i - 1, 0), last)]

    @pl.when((i == 0) | (cur != prev))
    def _():
        w1b_ref[...] = w1_ref[...].astype(BF16)
        w3b_ref[...] = w3_ref[...].astype(BF16)
        w2b_ref[...] = w2_ref[...].astype(BF16)

    @pl.when(i <= last)
    def _():
        x = x_ref[...]
        h1 = _dot(x, w1b_ref[...])
        h3 = _dot(x, w3b_ref[...])
        a = (h1 * _sigmoid(h1) * h3).astype(BF16)
        y = _dot(a, w2b_ref[...])
        D = y.shape[-1]
        hi = y.astype(BF16)
        o_ref[:, :D] = hi
        o_ref[:, D:] = (y - hi.astype(F32)).astype(BF16)


def _experts(buf, block_expert, n_used, w1, w3, w2, layer):
    R, D = buf.shape
    FF = w1.shape[-1]
    nb = R // MOE_BLOCK
    blk = lambda i, be, nu: (jnp.minimum(i, nu[0] - 1), 0)
    wsel = lambda i, be, nu: (layer, be[jnp.minimum(i, nu[0] - 1)], 0, 0)
    return pl.pallas_call(
        _expert_kernel,
        grid_spec=pltpu.PrefetchScalarGridSpec(
            num_scalar_prefetch=2,
            grid=(nb,),
            in_specs=[
                pl.BlockSpec((MOE_BLOCK, D), blk),
                pl.BlockSpec((None, None, D, FF), wsel),
                pl.BlockSpec((None, None, D, FF), wsel),
                pl.BlockSpec((None, None, FF, D), wsel),
            ],
            out_specs=pl.BlockSpec((MOE_BLOCK, 2 * D), blk),
            scratch_shapes=[pltpu.VMEM((D, FF), BF16), pltpu.VMEM((D, FF), BF16), pltpu.VMEM((FF, D), BF16)],
        ),
        out_shape=jax.ShapeDtypeStruct((R, 2 * D), BF16),
        compiler_params=_cparams(("arbitrary",)),
        name="expert_mlp",
    )(block_expert, n_used, buf, w1, w3, w2)


def _final_kernel(x_ref, r0_ref, r1_ref, rt_ref, g_ref, o_ref):
    o_ref[...] = _rms(_moe_combine(x_ref, r0_ref, r1_ref, rt_ref), g_ref[...])


def _final_norm(x2, moe, g):
    T, D = x2.shape
    tm = min(ROW_TILE, T)
    row = lambda n: pl.BlockSpec((tm, n), lambda i: (i, 0))
    return pl.pallas_call(
        _final_kernel,
        grid=(T // tm,),
        in_specs=[row(D), row(2 * D), row(2 * D), row(LANES), pl.BlockSpec((1, D), lambda i: (0, 0))],
        out_specs=row(D),
        out_shape=jax.ShapeDtypeStruct((T, D), F32),
        compiler_params=_cparams(("parallel",)),
        name="final_norm",
    )(x2, *moe, g)


def _pack_plan(D):
    widths = (D_ATTN, D_ATTN, D_ATTN, ATTN_HEADS, D_SSD, CONV_CH, SSD_HEADS, D, D)
    src = dict(zip(("q", "k", "v", "f", "z", "xbc", "dt", "ga", "gs"),
                   zip(np.concatenate([[0], np.cumsum(widths)[:-1]]).tolist(), widths)))
    plan, dst = [], 0
    for name in ("q", "k", "v", "z", "xbc", "ga", "gs", "f", "dt"):
        s0, w = src[name]
        plan.append((s0, w, dst, ATTN_HEAD_DIM ** -0.5 * LOG2E if name == "q" else 1.0))
        dst += w
    return plan, dst


def _pack_kernel(w_ref, o_ref, *, plan, used):
    for s0, w, d0, scale in plan:
        piece = w_ref[:, s0:s0 + w]
        if scale != 1.0:
            piece = piece * scale
        o_ref[:, d0:d0 + w] = piece.astype(BF16)
    o_ref[:, used:] = jnp.zeros((o_ref.shape[0], o_ref.shape[1] - used), BF16)


def _pack_w_in(w_in):
    depth, D, cols = w_in.shape
    plan, used = _pack_plan(D)
    n_out = -(-used // LANES) * LANES
    rows = 128
    return pl.pallas_call(
        functools.partial(_pack_kernel, plan=plan, used=used),
        grid=(depth, D // rows),
        in_specs=[pl.BlockSpec((None, rows, cols), lambda l, r: (l, r, 0))],
        out_specs=pl.BlockSpec((None, rows, n_out), lambda l, r: (l, r, 0)),
        out_shape=jax.ShapeDtypeStruct((depth, D, n_out), BF16),
        compiler_params=_cparams(("parallel", "parallel")),
        name="pack_w_in",
    )(w_in)


def _small_rows(mat, lane0):
    depth, n = mat.shape
    return jnp.zeros((depth, 1, SMALL_W), F32).at[:, 0, lane0:lane0 + n].set(mat.astype(F32))


def kernel(x, norm_mix_g, w_in, b_f, conv_w, conv_b, dt_bias, a_log, d_skip, ssd_norm_g, w_br_attn, w_br_ssd,
           w_out, norm_ffn_g, w_group_router, b_group_router, w_expert_router, b_expert_router, w1, w3, w2,
           final_g):
    B, S, D = x.shape
    T = B * S
    depth = w_in.shape[0]
    w_packed = _pack_w_in(w_in)
    bf_rows = _small_rows(b_f, 0)
    dtb_rows = _small_rows(dt_bias, HEAD_LANE0)
    alog_rows = _small_rows(a_log, HEAD_LANE0)
    dskip_rows = jnp.repeat(d_skip, SSD_HEAD_DIM, axis=1)[:, None, :]
    router_pad = LANES - N_EXPERT_GROUPS - N_EXPERTS
    w_router = jnp.concatenate([w_group_router, w_expert_router, jnp.zeros((depth, D, router_pad), F32)], axis=2)
    b_router = jnp.concatenate([b_group_router, b_expert_router, jnp.zeros((depth, router_pad), F32)],
                               axis=1)[:, None, :]
    wa, ws, wo = w_br_attn.astype(BF16), w_br_ssd.astype(BF16), w_out.astype(BF16)

    def trunk(xb):
        Bb = xb.shape[0]
        Tb = Bb * S
        x2 = xb.reshape(Tb, D)
        moe = None
        for l in range(depth):
            x2, (qkv, z, xbc, gates, small) = _inproj(x2, moe, norm_mix_g[l][None, :], w_packed, l)
            small3 = small.reshape(Bb, S, SMALL_W)
            fk = _fcum(small3, bf_rows[l])
            y_attn = _attention(qkv.reshape(Bb, S, 3 * D_ATTN), fk).reshape(Tb, D_ATTN)
            y_ssd = _ssd(xbc.reshape(Bb, S, CONV_CH), z.reshape(Bb, S, D_SSD), small3, conv_w[l],
                         conv_b[l][None, :], dtb_rows[l], alog_rows[l], dskip_rows[l],
                         ssd_norm_g[l][None, :]).reshape(Tb, D_SSD)
            x2, h2, logits = _merge(y_attn, y_ssd, gates, x2, wa[l], ws[l], wo[l], norm_ffn_g[l][None, :],
                                    w_router[l], b_router[l])
            route, cnt = _route(logits)
            dest0, dest1, src_tok, block_expert, n_used = _dispatch_plan(route, cnt)
            out = _experts(h2[src_tok], block_expert, n_used, w1, w3, w2, l)
            moe = (out[dest0], out[dest1], route)
        return _final_norm(x2, moe, final_g[None, :]).reshape(Bb, S, D)

    return trunk(x)
```

```python
import functools

import numpy as np
import jax
import jax.numpy as jnp
from jax import lax
from jax.experimental import pallas as pl
from jax.experimental.pallas import tpu as pltpu

F32 = jnp.float32
BF16 = jnp.bfloat16

ATTN_HEADS = 8
ATTN_HEAD_DIM = 64
D_ATTN = ATTN_HEADS * ATTN_HEAD_DIM
SSD_HEADS = 16
SSD_HEAD_DIM = 64
D_SSD = SSD_HEADS * SSD_HEAD_DIM
SSD_GROUPS = 2
SSD_STATE = 128
CONV_WIDTH = 4
CONV_CH = D_SSD + 2 * SSD_GROUPS * SSD_STATE
N_EXPERT_GROUPS = 4
EXPERTS_PER_GROUP = 8
N_EXPERTS = N_EXPERT_GROUPS * EXPERTS_PER_GROUP
TOP_K = 2
EPS = 1e-6

LANES = 128
HEAD_LANE0 = 8
SMALL_W = LANES
NEG_BIG = -1e30
LOG2E = 1.4426950408889634
VMEM_LIMIT = 56 * 1024 * 1024

SSD_CHUNK = 128
SSD_CHUNKS_PER_STEP = 2
CONV_HALO = 8
ATTN_TILE = 1024
DIAG_SPLIT = 4
FCUM_TILE = 512
ROW_TILE = 512
PROJ_COLS = 512
PACK_ROWS = 128
MOE_BLOCK = 256
R_E0, R_E1, R_G0, R_G1, R_RANK0, R_RANK1 = range(6)


def _cparams(sem):
    return pltpu.CompilerParams(dimension_semantics=sem, vmem_limit_bytes=VMEM_LIMIT)


def _split3(x):
    hi = x.astype(BF16)
    r = x - hi.astype(F32)
    mid = r.astype(BF16)
    lo = (r - mid.astype(F32)).astype(BF16)
    return hi, mid, lo


def _dot(a, b):
    return jnp.dot(a, b, preferred_element_type=F32)


def _dot_nt(a, b):
    return lax.dot_general(a, b, (((1,), (1,)), ((), ())), preferred_element_type=F32)


def _dot_x_01(x, m01):
    hi, mid, lo = _split3(x)
    return _dot(hi, m01) + _dot(mid, m01) + _dot(lo, m01)


def _dot_01_x(m01, x):
    hi, mid, lo = _split3(x)
    return _dot(m01, hi) + _dot(m01, mid) + _dot(m01, lo)


def _tril01(n, strict=False):
    r = lax.broadcasted_iota(jnp.int32, (n, n), 0)
    c = lax.broadcasted_iota(jnp.int32, (n, n), 1)
    return ((c < r) if strict else (c <= r)).astype(BF16)


def _softplus(x):
    return jnp.maximum(x, 0.0) + jnp.log1p(jnp.exp(-jnp.abs(x)))


def _sigmoid(x):
    return 0.5 * jnp.tanh(0.5 * x) + 0.5


def _rms(x, g):
    ms = jnp.mean(x * x, axis=-1, keepdims=True)
    return x * lax.rsqrt(ms + EPS) * g


def _moe_combine(x_ref, r0_ref, r1_ref, rt_ref):
    rt = rt_ref[...]
    return x_ref[...] + (rt[:, R_G0:R_G0 + 1] * r0_ref[...] + rt[:, R_G1:R_G1 + 1] * r1_ref[...])


def _inproj_kernel(*refs, with_moe):
    if with_moe:
        x_ref, r0_ref, r1_ref, rt_ref, g_ref, w_ref, xo_ref, *outs = refs
        x = _moe_combine(x_ref, r0_ref, r1_ref, rt_ref)
        xo_ref[...] = x
    else:
        x_ref, g_ref, w_ref, *outs = refs
        x = x_ref[...]
    h = _rms(x, g_ref[...]).astype(BF16)
    off = 0
    for ref in outs:
        n = ref.shape[-1]
        for c0 in range(0, n, PROJ_COLS):
            c1 = min(c0 + PROJ_COLS, n)
            ref[:, c0:c1] = _dot(h, w_ref[:, off + c0:off + c1]).astype(ref.dtype)
        off += n


def _inproj(x2, moe, g, w_packed, layer):
    T, D = x2.shape
    widths = (3 * D_ATTN, D_SSD, CONV_CH, 2 * D, SMALL_W)
    dtypes = (BF16, F32, F32, F32, F32)
    tm = min(ROW_TILE, T)
    row = lambda n: pl.BlockSpec((tm, n), lambda i: (i, 0))
    with_moe = moe is not None
    ins = [x2] + (list(moe) if with_moe else []) + [g, w_packed]
    in_specs = [row(D)] + ([row(D), row(D), row(LANES)] if with_moe else []) + [
        pl.BlockSpec((1, D), lambda i: (0, 0)),
        pl.BlockSpec((None,) + w_packed.shape[1:], lambda i: (layer, 0, 0), pipeline_mode=pl.Buffered(1)),
    ]
    out_specs = ([row(D)] if with_moe else []) + [row(n) for n in widths]
    out_shape = ([jax.ShapeDtypeStruct((T, D), F32)] if with_moe else []) + [
        jax.ShapeDtypeStruct((T, n), dt) for n, dt in zip(widths, dtypes)]
    res = pl.pallas_call(
        functools.partial(_inproj_kernel, with_moe=with_moe),
        grid=(T // tm,),
        in_specs=in_specs,
        out_specs=out_specs,
        out_shape=out_shape,
        compiler_params=_cparams(("parallel",)),
        name="inproj",
    )(*ins)
    return (res[0], res[1:]) if with_moe else (x2, res)


N_FPARTS = 3
HEAD_PAIRS = D_ATTN // LANES


def _fcum_kernel(s_ref, b_ref, place_ref, fk_ref, carry_ref):
    c = pl.program_id(1)

    @pl.when(c == 0)
    def _():
        carry_ref[...] = jnp.zeros_like(carry_ref)

    x = s_ref[...] + b_ref[...]
    logf = jnp.minimum(x, 0.0) - jnp.log1p(jnp.exp(-jnp.abs(x)))
    n = x.shape[0]
    cum = _dot_01_x(_tril01(n), logf) + carry_ref[0:1, :]
    carry_ref[0:1, :] = cum[n - 1:n, :]
    parts = jnp.concatenate(_split3(cum * (-LOG2E)), axis=1)
    for p in range(HEAD_PAIRS):
        fk_ref[p] = _dot(parts, place_ref[p]).astype(BF16)


def _fpart_placement():
    m = np.zeros((HEAD_PAIRS, N_FPARTS * LANES, LANES), np.float32)
    for p in range(HEAD_PAIRS):
        for j in range(N_FPARTS):
            m[p, j * LANES + 2 * p, ATTN_HEAD_DIM + j] = 1.0
            m[p, j * LANES + 2 * p + 1, j] = 1.0
    return jnp.asarray(m, BF16)


def _fcum(small3, bf_row):
    B, S, _ = small3.shape
    L = min(FCUM_TILE, S)
    place = _fpart_placement()
    return pl.pallas_call(
        _fcum_kernel,
        grid=(B, S // L),
        in_specs=[
            pl.BlockSpec((None, L, SMALL_W), lambda b, c: (b, c, 0)),
            pl.BlockSpec((1, SMALL_W), lambda b, c: (0, 0)),
            pl.BlockSpec(place.shape, lambda b, c: (0, 0, 0)),
        ],
        out_specs=pl.BlockSpec((None, HEAD_PAIRS, L, LANES), lambda b, c: (b, 0, c, 0)),
        out_shape=jax.ShapeDtypeStruct((B, HEAD_PAIRS, S, LANES), BF16),
        scratch_shapes=[pltpu.VMEM((8, SMALL_W), F32)],
        compiler_params=_cparams(("parallel", "arbitrary")),
        name="forget_cumsum",
    )(small3, bf_row, place)


def _attn_kernel(q_ref, k_ref, v_ref, fk_ref, o_ref, ka_ref, kb_ref, va_ref, vb_ref, s_ref, p_ref, m_ref,
                 acc_ref, *, t):
    i = pl.program_id(2)
    lane = lax.broadcasted_iota(jnp.int32, (1, LANES), 1)
    lo = lane < ATTN_HEAD_DIM
    den_lane = (ATTN_HEAD_DIM, 0)
    ones_at = lambda cond: jnp.where(cond, 1.0, 0.0).astype(BF16)

    @pl.when(i == 0)
    def _():
        k = k_ref[...]
        fk = fk_ref[...]
        ka_ref[...] = jnp.where(lo, k, fk)
        kb_ref[...] = jnp.where(lo, fk, k)
        v = v_ref[...]
        va_ref[...] = jnp.where(lo, v, ones_at(lane == den_lane[0]))
        vb_ref[...] = jnp.where(lo, ones_at(lane == den_lane[1]), v)

    q = q_ref[...]
    bias_a = ones_at((lane >= ATTN_HEAD_DIM) & (lane < ATTN_HEAD_DIM + N_FPARTS))
    bias_b = ones_at(lane < N_FPARTS)
    q_heads = (jnp.where(lo, q, bias_a), jnp.where(lo, bias_b, q))
    k_refs = (ka_ref, kb_ref)
    v_refs = (va_ref, vb_ref)
    m_ref[...] = jnp.full(m_ref.shape, NEG_BIG, F32)
    acc_ref[...] = jnp.zeros(acc_ref.shape, F32)

    def step(kb, r0, nr, nc, masked):
        ks = pl.multiple_of(kb * t, t)
        rows = slice(r0, r0 + nr)
        for e in range(2):
            s = _dot_nt(q_heads[e][rows], k_refs[e][pl.ds(ks, nc), :])
            if masked:
                r = r0 + lax.broadcasted_iota(jnp.int32, (nr, nc), 0)
                c = lax.broadcasted_iota(jnp.int32, (nr, nc), 1)
                s = jnp.where(c <= r, s, NEG_BIG)
            s_ref[e, rows, 0:nc] = s
        for e in range(2):
            m_prev = m_ref[e, rows, :]
            m_new = jnp.maximum(m_prev, jnp.max(s_ref[e, rows, 0:nc], axis=-1, keepdims=True))
            m_ref[e, rows, :] = m_new
            p_ref[e, rows, 0:nc] = jnp.exp2(
                s_ref[e, rows, 0:nc] - jnp.concatenate([m_new] * (nc // LANES), axis=1)).astype(BF16)
            pv = _dot(p_ref[e, rows, 0:nc], v_refs[e][pl.ds(ks, nc), :])
            acc_ref[e, rows, :] = acc_ref[e, rows, :] * jnp.exp2(m_prev - m_new) + pv

    def body(kb, carry):
        step(kb, 0, t, t, False)
        return carry

    lax.fori_loop(0, i, body, 0)
    nr = t // DIAG_SPLIT
    for r in range(DIAG_SPLIT):
        step(i, r * nr, nr, (r + 1) * nr, True)
    acc_a = acc_ref[0]
    acc_b = acc_ref[1]
    inv_a = 1.0 / acc_a[:, den_lane[0]:den_lane[0] + 1]
    inv_b = 1.0 / acc_b[:, den_lane[1]:den_lane[1] + 1]
    o_ref[...] = jnp.where(lo, acc_a * inv_a, acc_b * inv_b).astype(o_ref.dtype)


def _attention(qkv3, fk):
    B, S, _ = qkv3.shape
    t = min(ATTN_TILE, S)
    seq = lambda: pltpu.VMEM((S, LANES), BF16)
    return pl.pallas_call(
        functools.partial(_attn_kernel, t=t),
        grid=(B, HEAD_PAIRS, S // t),
        in_specs=[
            pl.BlockSpec((None, t, LANES), lambda b, h, i: (b, i, h)),
            pl.BlockSpec((None, S, LANES), lambda b, h, i: (b, 0, HEAD_PAIRS + h)),
            pl.BlockSpec((None, S, LANES), lambda b, h, i: (b, 0, 2 * HEAD_PAIRS + h)),
            pl.BlockSpec((None, None, S, LANES), lambda b, h, i: (b, h, 0, 0)),
        ],
        out_specs=pl.BlockSpec((None, t, LANES), lambda b, h, i: (b, i, h)),
        out_shape=jax.ShapeDtypeStruct((B, S, D_ATTN), BF16),
        scratch_shapes=[
            seq(), seq(), seq(), seq(),
            pltpu.VMEM((2, t, t), F32), pltpu.VMEM((2, t, t), BF16),
            pltpu.VMEM((2, t, LANES), F32), pltpu.VMEM((2, t, LANES), F32),
        ],
        compiler_params=_cparams(("parallel", "parallel", "arbitrary")),
        name="fox_attention",
    )(qkv3, qkv3, qkv3, fk)


def _ssd_kernel(*refs, L, chunks):
    for j in range(chunks):
        _ssd_chunk(*refs, L=L, j=j)


def _ssd_chunk(xbc_ref, z_ref, small_ref, cw_ref, cb_ref, dtb_ref, alog_ref, dskip_ref, ng_ref,
               e1_ref, o_ref, xbuf_ref, state_ref, *, L, j):
    c = pl.program_id(1)
    halo = CONV_HALO
    rows = slice(j * L, (j + 1) * L)

    def carry_halo():
        xbuf_ref[0:halo, :] = xbuf_ref[L:L + halo, :]

    if j == 0:
        @pl.when(c == 0)
        def _():
            xbuf_ref[0:halo, :] = jnp.zeros((halo, CONV_CH), F32)
            state_ref[...] = jnp.zeros_like(state_ref)

        pl.when(c > 0)(carry_halo)
    else:
        carry_halo()

    xbuf_ref[halo:halo + L, :] = xbc_ref[rows, :]
    w = cw_ref[...]
    conv = cb_ref[...] + w[3:4, :] * xbuf_ref[halo:halo + L, :]
    for kk in range(CONV_WIDTH - 1):
        conv = conv + w[kk:kk + 1, :] * xbuf_ref[pl.ds(halo - (CONV_WIDTH - 1) + kk, L), :]
    u = conv * _sigmoid(conv)
    xs = u[:, :D_SSD]
    bm = u[:, D_SSD:D_SSD + SSD_GROUPS * SSD_STATE]
    cm = u[:, D_SSD + SSD_GROUPS * SSD_STATE:]

    dt = _softplus(small_ref[rows, :] + dtb_ref[...])
    a = dt * (-jnp.exp(alog_ref[...]))
    acs = _dot_01_x(_tril01(L), a)
    e1 = e1_ref[...]
    dt_e = _dot_x_01(dt, e1)
    acs_e = _dot_x_01(acs, e1)
    acs_t = acs.T
    last_e = acs_e[L - 1:L, :]
    x_dt = xs * dt_e
    x_dt_bf = x_dt.astype(BF16)
    x_end_bf = (x_dt * jnp.exp(last_e - acs_e)).astype(BF16)
    grow = jnp.exp(acs_e)
    chunk_decay = jnp.exp(last_e)

    row = lax.broadcasted_iota(jnp.int32, (L, L), 0)
    col = lax.broadcasted_iota(jnp.int32, (L, L), 1)
    causal = col <= row
    lo = lax.broadcasted_iota(jnp.int32, (1, LANES), 1) < SSD_HEAD_DIM
    pairs_per_group = SSD_HEADS // SSD_GROUPS // 2

    ys = []
    for g in range(SSD_GROUPS):
        bg = bm[:, g * SSD_STATE:(g + 1) * SSD_STATE]
        bg_bf = bg.astype(BF16)
        bg_t_bf = bg.T.astype(BF16)
        cg_bf = cm[:, g * SSD_STATE:(g + 1) * SSD_STATE].astype(BF16)
        cb = _dot_nt(cg_bf, bg_bf)
        for jj in range(pairs_per_group):
            j = g * pairs_per_group + jj
            sl = slice(j * LANES, (j + 1) * LANES)
            xp = x_dt_bf[:, sl]
            yd = []
            for e in range(2):
                h = 2 * j + e
                hl = HEAD_LANE0 + h
                seg = jnp.broadcast_to(acs[:, hl:hl + 1], (L, L)) - acs_t[hl:hl + 1, :]
                dec = jnp.exp(jnp.where(causal, seg, NEG_BIG))
                yd.append(_dot((cb * dec).astype(BF16), xp))
            st = state_ref[j]
            y_off = _dot(cg_bf, st.astype(BF16)) * grow[:, sl]
            ys.append(jnp.where(lo, yd[0], yd[1]) + y_off)
            state_ref[j] = st * chunk_decay[:, sl] + _dot(bg_t_bf, x_end_bf[:, sl])

    y = jnp.concatenate(ys, axis=1) + xs * dskip_ref[...]
    zz = z_ref[rows, :]
    y = y * (zz * _sigmoid(zz))
    o_ref[rows, :] = _rms(y, ng_ref[...]).astype(o_ref.dtype)


def _head_expand_mat():
    e1 = np.zeros((LANES, D_SSD), np.float32)
    for h in range(SSD_HEADS):
        e1[HEAD_LANE0 + h, h * SSD_HEAD_DIM:(h + 1) * SSD_HEAD_DIM] = 1.0
    return jnp.asarray(e1, BF16)


def _ssd(xbc3, z3, small3, conv_w, conv_b, dtb_row, alog_row, dskip_row, norm_g):
    B, S, _ = xbc3.shape
    L = min(SSD_CHUNK, S)
    chunks = SSD_CHUNKS_PER_STEP if S % (SSD_CHUNKS_PER_STEP * L) == 0 else 1
    rows = chunks * L
    e1 = _head_expand_mat()
    const = lambda shape: pl.BlockSpec(shape, lambda b, c: (0,) * len(shape))
    return pl.pallas_call(
        functools.partial(_ssd_kernel, L=L, chunks=chunks),
        grid=(B, S // rows),
        in_specs=[
            pl.BlockSpec((None, rows, CONV_CH), lambda b, c: (b, c, 0)),
            pl.BlockSpec((None, rows, D_SSD), lambda b, c: (b, c, 0)),
            pl.BlockSpec((None, rows, SMALL_W), lambda b, c: (b, c, 0)),
            const((CONV_WIDTH, CONV_CH)), const((1, CONV_CH)), const((1, SMALL_W)), const((1, SMALL_W)),
            const((1, D_SSD)), const((1, D_SSD)), const(e1.shape),
        ],
        out_specs=pl.BlockSpec((None, rows, D_SSD), lambda b, c: (b, c, 0)),
        out_shape=jax.ShapeDtypeStruct((B, S, D_SSD), BF16),
        scratch_shapes=[pltpu.VMEM((L + CONV_HALO, CONV_CH), F32),
                        pltpu.VMEM((SSD_HEADS // 2, SSD_STATE, LANES), F32)],
        compiler_params=_cparams(("parallel", "arbitrary")),
        name="conv_ssd",
    )(xbc3, z3, small3, conv_w, conv_b, dtb_row, alog_row, dskip_row, norm_g, e1)


def _merge_kernel(ya_ref, ys_ref, gates_ref, x_ref, wa_ref, ws_ref, wo_ref, g2_ref, wr_ref, br_ref,
                  xo_ref, h_ref, lg_ref):
    D = x_ref.shape[-1]
    pa = _dot(ya_ref[...], wa_ref[...])
    ps = _dot(ys_ref[...], ws_ref[...])
    merged = _sigmoid(gates_ref[:, :D]) * pa + _sigmoid(gates_ref[:, D:]) * ps
    xn = x_ref[...] + _dot(merged.astype(BF16), wo_ref[...])
    xo_ref[...] = xn
    h = _rms(xn, g2_ref[...])
    h_ref[...] = h.astype(BF16)
    h_hi, h_mid, _ = _split3(h)
    w_hi, w_mid, _ = _split3(wr_ref[...])
    lg_ref[...] = _dot(h_hi, w_hi) + _dot(h_hi, w_mid) + _dot(h_mid, w_hi) + br_ref[...]


def _merge(ya, ys, gates, x2, wa, ws, wo, g2, wr, br):
    T, D = x2.shape
    tm = min(ROW_TILE, T)
    row = lambda n: pl.BlockSpec((tm, n), lambda i: (i, 0))
    const = lambda a: pl.BlockSpec(a.shape, lambda i: (0, 0))
    return pl.pallas_call(
        _merge_kernel,
        grid=(T // tm,),
        in_specs=[row(D_ATTN), row(D_SSD), row(2 * D), row(D),
                  const(wa), const(ws), const(wo), const(g2), const(wr), const(br)],
        out_specs=[row(D), row(D), row(LANES)],
        out_shape=[jax.ShapeDtypeStruct((T, D), F32), jax.ShapeDtypeStruct((T, D), BF16),
                   jax.ShapeDtypeStruct((T, LANES), F32)],
        compiler_params=_cparams(("parallel",)),
        name="merge_outproj_router",
    )(ya, ys, gates, x2, wa, ws, wo, g2, wr, br)


def _route_kernel(lg_ref, route_ref, cnt_ref, carry_ref):
    i = pl.program_id(0)

    @pl.when(i == 0)
    def _():
        carry_ref[...] = jnp.zeros_like(carry_ref)

    lg = lg_ref[...]
    tm = lg.shape[0]
    lane_i = lax.broadcasted_iota(jnp.int32, (tm, LANES), 1)
    lane = lane_i.astype(F32)
    first = lambda cond: jnp.min(jnp.where(cond, lane, float(LANES)), axis=-1, keepdims=True)

    gmask = lane_i < N_EXPERT_GROUPS
    gl = jnp.where(gmask, lg, NEG_BIG)
    gmax = jnp.max(gl, axis=-1, keepdims=True)
    gsum = jnp.sum(jnp.where(gmask, jnp.exp(gl - gmax), 0.0), axis=-1, keepdims=True)
    g_idx = first(gmask & (gl == gmax))
    g_w = 1.0 / gsum

    e_lo = N_EXPERT_GROUPS + g_idx * EXPERTS_PER_GROUP
    emask = (lane >= e_lo) & (lane < e_lo + EXPERTS_PER_GROUP)
    el = jnp.where(emask, lg, NEG_BIG)
    emax = jnp.max(el, axis=-1, keepdims=True)
    esum = jnp.sum(jnp.where(emask, jnp.exp(el - emax), 0.0), axis=-1, keepdims=True)
    l0 = first(emask & (el == emax))
    el2 = jnp.where(lane == l0, NEG_BIG, el)
    m2 = jnp.max(el2, axis=-1, keepdims=True)
    l1 = first(emask & (lane != l0) & (el2 == m2))
    gate0 = g_w * (1.0 / esum)
    gate1 = g_w * (jnp.exp(m2 - emax) / esum)

    is0 = lane == l0
    is1 = lane == l1
    onehot = jnp.where(is0 | is1, 1.0, 0.0)
    before = _dot(_tril01(tm, strict=True), onehot.astype(BF16)) + carry_ref[0:1, :]
    rank0 = jnp.sum(jnp.where(is0, before, 0.0), axis=-1, keepdims=True)
    rank1 = jnp.sum(jnp.where(is1, before, 0.0), axis=-1, keepdims=True)
    total = carry_ref[0:1, :] + jnp.sum(onehot, axis=0, keepdims=True)
    carry_ref[0:1, :] = total
    cnt_ref[...] = jnp.broadcast_to(total, cnt_ref.shape)

    rec = jnp.zeros((tm, LANES), F32)
    for ln, val in ((R_E0, l0 - N_EXPERT_GROUPS), (R_E1, l1 - N_EXPERT_GROUPS), (R_G0, gate0), (R_G1, gate1),
                    (R_RANK0, rank0), (R_RANK1, rank1)):
        rec = jnp.where(lane_i == ln, val, rec)
    route_ref[...] = rec


def _route(logits):
    T = logits.shape[0]
    tm = min(ROW_TILE, T)
    return pl.pallas_call(
        _route_kernel,
        grid=(T // tm,),
        in_specs=[pl.BlockSpec((tm, LANES), lambda i: (i, 0))],
        out_specs=[pl.BlockSpec((tm, LANES), lambda i: (i, 0)), pl.BlockSpec((8, LANES), lambda i: (0, 0))],
        out_shape=[jax.ShapeDtypeStruct((T, LANES), F32), jax.ShapeDtypeStruct((8, LANES), F32)],
        scratch_shapes=[pltpu.VMEM((8, LANES), F32)],
        compiler_params=_cparams(("arbitrary",)),
        name="route",
    )(logits)


def _table_lookup(idx, table):
    n = table.shape[0]
    return jnp.sum(jnp.where(idx[:, None] == jnp.arange(n, dtype=idx.dtype)[None, :], table[None, :], 0), axis=1)


def _dispatch_plan(route, cnt):
    T = route.shape[0]
    counts = cnt[0, N_EXPERT_GROUPS:N_EXPERT_GROUPS + N_EXPERTS].astype(jnp.int32)
    padded = (counts + MOE_BLOCK - 1) // MOE_BLOCK * MOE_BLOCK
    pad_end = jnp.cumsum(padded)
    pad_start = pad_end - padded
    start = jnp.cumsum(counts) - counts
    e0 = route[:, R_E0].astype(jnp.int32)
    e1 = route[:, R_E1].astype(jnp.int32)
    dest0 = _table_lookup(e0, pad_start) + route[:, R_RANK0].astype(jnp.int32)
    dest1 = _table_lookup(e1, pad_start) + route[:, R_RANK1].astype(jnp.int32)
    n_blocks = -(-(T * TOP_K) // MOE_BLOCK) + N_EXPERTS
    block_end = pad_end // MOE_BLOCK
    blk = jnp.arange(n_blocks, dtype=jnp.int32)
    block_expert = jnp.minimum(jnp.sum(blk[:, None] >= block_end[None, :], axis=1), N_EXPERTS - 1).astype(jnp.int32)
    n_used = block_end[-1:].astype(jnp.int32)
    order = jnp.argsort(jnp.stack([e0, e1], axis=1).reshape(-1), stable=True).astype(jnp.int32)
    shift = _table_lookup(block_expert, start - pad_start)
    pos = jnp.arange(n_blocks * MOE_BLOCK, dtype=jnp.int32) + jnp.repeat(shift, MOE_BLOCK)
    src_tok = order[jnp.clip(pos, 0, T * TOP_K - 1)] // TOP_K
    return dest0, dest1, src_tok, block_expert, n_used


def _expert_kernel(be_ref, nu_ref, x_ref, w1_ref, w3_ref, w2_ref, o_ref, w1b_ref, w3b_ref, w2b_ref):
    i = pl.program_id(0)
    last = nu_ref[0] - 1
    cur = be_ref[jnp.minimum(i, last)]
    prev = be_ref[jnp.minimum(jnp.maximum(i - 1, 0), last)]

    @pl.when((i == 0) | (cur != prev))
    def _():
        w1b_ref[...] = w1_ref[...].astype(BF16)
        w3b_ref[...] = w3_ref[...].astype(BF16)
        w2b_ref[...] = w2_ref[...].astype(BF16)

    @pl.when(i <= last)
    def _():
        x = x_ref[...]
        h1 = _dot(x, w1b_ref[...])
        h3 = _dot(x, w3b_ref[...])
        a = (h1 * _sigmoid(h1) * h3).astype(BF16)
        o_ref[...] = _dot(a, w2b_ref[...])


def _experts(buf, block_expert, n_used, w1, w3, w2, layer):
    R, D = buf.shape
    FF = w1.shape[-1]
    nb = R // MOE_BLOCK
    blk = lambda i, be, nu: (jnp.minimum(i, nu[0] - 1), 0)
    wsel = lambda i, be, nu: (layer, be[jnp.minimum(i, nu[0] - 1)], 0, 0)
    return pl.pallas_call(
        _expert_kernel,
        grid_spec=pltpu.PrefetchScalarGridSpec(
            num_scalar_prefetch=2,
            grid=(nb,),
            in_specs=[
                pl.BlockSpec((MOE_BLOCK, D), blk),
                pl.BlockSpec((None, None, D, FF), wsel),
                pl.BlockSpec((None, None, D, FF), wsel),
                pl.BlockSpec((None, None, FF, D), wsel),
            ],
            out_specs=pl.BlockSpec((MOE_BLOCK, D), blk),
            scratch_shapes=[pltpu.VMEM((D, FF), BF16), pltpu.VMEM((D, FF), BF16), pltpu.VMEM((FF, D), BF16)],
        ),
        out_shape=jax.ShapeDtypeStruct((R, D), F32),
        compiler_params=_cparams(("arbitrary",)),
        name="expert_mlp",
    )(block_expert, n_used, buf, w1, w3, w2)


def _final_kernel(x_ref, r0_ref, r1_ref, rt_ref, g_ref, o_ref):
    o_ref[...] = _rms(_moe_combine(x_ref, r0_ref, r1_ref, rt_ref), g_ref[...])


def _final_norm(x2, moe, g):
    T, D = x2.shape
    tm = min(ROW_TILE, T)
    row = lambda n: pl.BlockSpec((tm, n), lambda i: (i, 0))
    return pl.pallas_call(
        _final_kernel,
        grid=(T // tm,),
        in_specs=[row(D), row(D), row(D), row(LANES), pl.BlockSpec((1, D), lambda i: (0, 0))],
        out_specs=row(D),
        out_shape=jax.ShapeDtypeStruct((T, D), F32),
        compiler_params=_cparams(("parallel",)),
        name="final_norm",
    )(x2, *moe, g)


def _pack_plan(D):
    widths = (D_ATTN, D_ATTN, D_ATTN, ATTN_HEADS, D_SSD, CONV_CH, SSD_HEADS, D, D)
    src = dict(zip(("q", "k", "v", "f", "z", "xbc", "dt", "ga", "gs"),
                   zip(np.concatenate([[0], np.cumsum(widths)[:-1]]).tolist(), widths)))
    plan, dst = [], 0
    for name in ("q", "k", "v", "z", "xbc", "ga", "gs", "f", "dt"):
        s0, w = src[name]
        plan.append((s0, w, dst, ATTN_HEAD_DIM ** -0.5 * LOG2E if name == "q" else 1.0))
        dst += w
    return plan, dst


def _pack_kernel(w_ref, o_ref, *, plan, used):
    for s0, w, d0, scale in plan:
        piece = w_ref[:, s0:s0 + w]
        if scale != 1.0:
            piece = piece * scale
        o_ref[:, d0:d0 + w] = piece.astype(BF16)
    o_ref[:, used:] = jnp.zeros((o_ref.shape[0], o_ref.shape[1] - used), BF16)


def _pack_w_in(w_in):
    depth, D, cols = w_in.shape
    plan, used = _pack_plan(D)
    n_out = -(-used // LANES) * LANES
    rows = PACK_ROWS
    return pl.pallas_call(
        functools.partial(_pack_kernel, plan=plan, used=used),
        grid=(depth, D // rows),
        in_specs=[pl.BlockSpec((None, rows, cols), lambda l, r: (l, r, 0))],
        out_specs=pl.BlockSpec((None, rows, n_out), lambda l, r: (l, r, 0)),
        out_shape=jax.ShapeDtypeStruct((depth, D, n_out), BF16),
        compiler_params=_cparams(("parallel", "parallel")),
        name="pack_w_in",
    )(w_in)


def _small_rows(mat, lane0):
    depth, n = mat.shape
    return jnp.zeros((depth, 1, SMALL_W), F32).at[:, 0, lane0:lane0 + n].set(mat.astype(F32))


def kernel(x, norm_mix_g, w_in, b_f, conv_w, conv_b, dt_bias, a_log, d_skip, ssd_norm_g, w_br_attn, w_br_ssd,
           w_out, norm_ffn_g, w_group_router, b_group_router, w_expert_router, b_expert_router, w1, w3, w2,
           final_g):
    B, S, D = x.shape
    T = B * S
    depth = w_in.shape[0]
    w_packed = _pack_w_in(w_in)
    bf_rows = _small_rows(b_f, 0)
    dtb_rows = _small_rows(dt_bias, HEAD_LANE0)
    alog_rows = _small_rows(a_log, HEAD_LANE0)
    dskip_rows = jnp.repeat(d_skip, SSD_HEAD_DIM, axis=1)[:, None, :]
    router_pad = LANES - N_EXPERT_GROUPS - N_EXPERTS
    w_router = jnp.concatenate([w_group_router, w_expert_router, jnp.zeros((depth, D, router_pad), F32)], axis=2)
    b_router = jnp.concatenate([b_group_router, b_expert_router, jnp.zeros((depth, router_pad), F32)],
                               axis=1)[:, None, :]
    wa, ws, wo = w_br_attn.astype(BF16), w_br_ssd.astype(BF16), w_out.astype(BF16)

    x2 = x.reshape(T, D)
    moe = None
    for l in range(depth):
        x2, (qkv, z, xbc, gates, small) = _inproj(x2, moe, norm_mix_g[l][None, :], w_packed, l)
        small3 = small.reshape(B, S, SMALL_W)
        fk = _fcum(small3, bf_rows[l])
        y_attn = _attention(qkv.reshape(B, S, 3 * D_ATTN), fk).reshape(T, D_ATTN)
        y_ssd = _ssd(xbc.reshape(B, S, CONV_CH), z.reshape(B, S, D_SSD), small3, conv_w[l], conv_b[l][None, :],
                     dtb_rows[l], alog_rows[l], dskip_rows[l], ssd_norm_g[l][None, :]).reshape(T, D_SSD)
        x2, h2, logits = _merge(y_attn, y_ssd, gates, x2, wa[l], ws[l], wo[l], norm_ffn_g[l][None, :],
                                w_router[l], b_router[l])
        route, cnt = _route(logits)
        dest0, dest1, src_tok, block_expert, n_used = _dispatch_plan(route, cnt)
        out = _experts(h2[src_tok], block_expert, n_used, w1, w3, w2, l)
        moe = (out[dest0], out[dest1], route)
    return _final_norm(x2, moe, final_g[None, :]).reshape(B, S, D)
```

```python
import functools

import numpy as np
import jax
import jax.numpy as jnp
from jax import lax
from jax.experimental import pallas as pl
from jax.experimental.pallas import tpu as pltpu

F32 = jnp.float32
BF16 = jnp.bfloat16

ATTN_HEADS = 8
ATTN_HEAD_DIM = 64
D_ATTN = ATTN_HEADS * ATTN_HEAD_DIM
SSD_HEADS = 16
SSD_HEAD_DIM = 64
D_SSD = SSD_HEADS * SSD_HEAD_DIM
SSD_GROUPS = 2
SSD_STATE = 128
CONV_WIDTH = 4
CONV_CH = D_SSD + 2 * SSD_GROUPS * SSD_STATE
N_EXPERT_GROUPS = 4
EXPERTS_PER_GROUP = 8
N_EXPERTS = N_EXPERT_GROUPS * EXPERTS_PER_GROUP
TOP_K = 2
EPS = 1e-6

LANES = 128
HEAD_LANE0 = 8
SMALL_W = LANES
NEG_BIG = -1e30
LOG2E = 1.4426950408889634
VMEM_LIMIT = 56 * 1024 * 1024

SSD_CHUNK = 128
SSD_CHUNKS_PER_STEP = 4
CONV_HALO = 8
ATTN_TILE = 1024
DIAG_SPLIT = 2
FCUM_TILE = 512
ROW_TILE = 512
PROJ_COLS = 512
PACK_ROWS = 128
MOE_BLOCK = 256
R_E0, R_E1, R_G0, R_G1, R_RANK0, R_RANK1 = range(6)


def _cparams(sem):
    return pltpu.CompilerParams(dimension_semantics=sem, vmem_limit_bytes=VMEM_LIMIT)


def _split3(x):
    hi = x.astype(BF16)
    r = x - hi.astype(F32)
    mid = r.astype(BF16)
    lo = (r - mid.astype(F32)).astype(BF16)
    return hi, mid, lo


def _dot(a, b):
    return jnp.dot(a, b, preferred_element_type=F32)


def _dot_nt(a, b):
    return lax.dot_general(a, b, (((1,), (1,)), ((), ())), preferred_element_type=F32)


def _dot_x_01(x, m01):
    hi, mid, lo = _split3(x)
    return _dot(hi, m01) + _dot(mid, m01) + _dot(lo, m01)


def _dot_01_x(m01, x):
    hi, mid, lo = _split3(x)
    return _dot(m01, hi) + _dot(m01, mid) + _dot(m01, lo)


def _tril01(n, strict=False):
    r = lax.broadcasted_iota(jnp.int32, (n, n), 0)
    c = lax.broadcasted_iota(jnp.int32, (n, n), 1)
    return ((c < r) if strict else (c <= r)).astype(BF16)


def _softplus(x):
    return jnp.maximum(x, 0.0) + jnp.log1p(jnp.exp(-jnp.abs(x)))


def _sigmoid(x):
    return 0.5 * jnp.tanh(0.5 * x) + 0.5


def _rms(x, g):
    ms = jnp.mean(x * x, axis=-1, keepdims=True)
    return x * lax.rsqrt(ms + EPS) * g


def _moe_combine(x_ref, r0_ref, r1_ref, rt_ref):
    rt = rt_ref[...]
    return x_ref[...] + (rt[:, R_G0:R_G0 + 1] * r0_ref[...] + rt[:, R_G1:R_G1 + 1] * r1_ref[...])


def _inproj_kernel(*refs, with_moe):
    if with_moe:
        x_ref, r0_ref, r1_ref, rt_ref, g_ref, w_ref, xo_ref, *outs = refs
        x = _moe_combine(x_ref, r0_ref, r1_ref, rt_ref)
        xo_ref[...] = x
    else:
        x_ref, g_ref, w_ref, *outs = refs
        x = x_ref[...]
    h = _rms(x, g_ref[...]).astype(BF16)
    off = 0
    for ref in outs:
        n = ref.shape[-1]
        for c0 in range(0, n, PROJ_COLS):
            c1 = min(c0 + PROJ_COLS, n)
            ref[:, c0:c1] = _dot(h, w_ref[:, off + c0:off + c1]).astype(ref.dtype)
        off += n


def _inproj(x2, moe, g, w_packed, layer):
    T, D = x2.shape
    widths = (3 * D_ATTN, D_SSD, CONV_CH, 2 * D, SMALL_W)
    dtypes = (BF16, F32, F32, F32, F32)
    tm = min(ROW_TILE, T)
    row = lambda n: pl.BlockSpec((tm, n), lambda i: (i, 0))
    with_moe = moe is not None
    ins = [x2] + (list(moe) if with_moe else []) + [g, w_packed]
    in_specs = [row(D)] + ([row(D), row(D), row(LANES)] if with_moe else []) + [
        pl.BlockSpec((1, D), lambda i: (0, 0)),
        pl.BlockSpec((None,) + w_packed.shape[1:], lambda i: (layer, 0, 0), pipeline_mode=pl.Buffered(1)),
    ]
    out_specs = ([row(D)] if with_moe else []) + [row(n) for n in widths]
    out_shape = ([jax.ShapeDtypeStruct((T, D), F32)] if with_moe else []) + [
        jax.ShapeDtypeStruct((T, n), dt) for n, dt in zip(widths, dtypes)]
    res = pl.pallas_call(
        functools.partial(_inproj_kernel, with_moe=with_moe),
        grid=(T // tm,),
        in_specs=in_specs,
        out_specs=out_specs,
        out_shape=out_shape,
        compiler_params=_cparams(("parallel",)),
        name="inproj",
    )(*ins)
    return (res[0], res[1:]) if with_moe else (x2, res)


N_FPARTS = 3
HEAD_PAIRS = D_ATTN // LANES


def _fcum_kernel(s_ref, b_ref, place_ref, fk_ref, carry_ref):
    c = pl.program_id(1)

    @pl.when(c == 0)
    def _():
        carry_ref[...] = jnp.zeros_like(carry_ref)

    x = s_ref[...] + b_ref[...]
    logf = jnp.minimum(x, 0.0) - jnp.log1p(jnp.exp(-jnp.abs(x)))
    n = x.shape[0]
    cum = _dot_01_x(_tril01(n), logf) + carry_ref[0:1, :]
    carry_ref[0:1, :] = cum[n - 1:n, :]
    parts = jnp.concatenate(_split3(cum * (-LOG2E)), axis=1)
    for p in range(HEAD_PAIRS):
        fk_ref[p] = _dot(parts, place_ref[p]).astype(BF16)


def _fpart_placement():
    m = np.zeros((HEAD_PAIRS, N_FPARTS * LANES, LANES), np.float32)
    for p in range(HEAD_PAIRS):
        for j in range(N_FPARTS):
            m[p, j * LANES + 2 * p, ATTN_HEAD_DIM + j] = 1.0
            m[p, j * LANES + 2 * p + 1, j] = 1.0
    return jnp.asarray(m, BF16)


def _fcum(small3, bf_row):
    B, S, _ = small3.shape
    L = min(FCUM_TILE, S)
    place = _fpart_placement()
    return pl.pallas_call(
        _fcum_kernel,
        grid=(B, S // L),
        in_specs=[
            pl.BlockSpec((None, L, SMALL_W), lambda b, c: (b, c, 0)),
            pl.BlockSpec((1, SMALL_W), lambda b, c: (0, 0)),
            pl.BlockSpec(place.shape, lambda b, c: (0, 0, 0)),
        ],
        out_specs=pl.BlockSpec((None, HEAD_PAIRS, L, LANES), lambda b, c: (b, 0, c, 0)),
        out_shape=jax.ShapeDtypeStruct((B, HEAD_PAIRS, S, LANES), BF16),
        scratch_shapes=[pltpu.VMEM((8, SMALL_W), F32)],
        compiler_params=_cparams(("parallel", "arbitrary")),
        name="forget_cumsum",
    )(small3, bf_row, place)


def _attn_kernel(q_ref, k_ref, v_ref, fk_ref, o_ref, ka_ref, kb_ref, va_ref, vb_ref, s_ref, p_ref, m_ref,
                 acc_ref, *, t):
    i = pl.program_id(2)
    lane = lax.broadcasted_iota(jnp.int32, (1, LANES), 1)
    lo = lane < ATTN_HEAD_DIM
    den_lane = (ATTN_HEAD_DIM, 0)
    ones_at = lambda cond: jnp.where(cond, 1.0, 0.0).astype(BF16)

    @pl.when(i == 0)
    def _():
        k = k_ref[...]
        fk = fk_ref[...]
        ka_ref[...] = jnp.where(lo, k, fk)
        kb_ref[...] = jnp.where(lo, fk, k)
        v = v_ref[...]
        va_ref[...] = jnp.where(lo, v, ones_at(lane == den_lane[0]))
        vb_ref[...] = jnp.where(lo, ones_at(lane == den_lane[1]), v)

    q = q_ref[...]
    bias_a = ones_at((lane >= ATTN_HEAD_DIM) & (lane < ATTN_HEAD_DIM + N_FPARTS))
    bias_b = ones_at(lane < N_FPARTS)
    q_heads = (jnp.where(lo, q, bias_a), jnp.where(lo, bias_b, q))
    k_refs = (ka_ref, kb_ref)
    v_refs = (va_ref, vb_ref)
    m_ref[...] = jnp.full(m_ref.shape, NEG_BIG, F32)
    acc_ref[...] = jnp.zeros(acc_ref.shape, F32)

    def step(kb, r0, nr, nc, masked):
        ks = pl.multiple_of(kb * t, t)
        rows = slice(r0, r0 + nr)
        for e in range(2):
            s = _dot_nt(q_heads[e][rows], k_refs[e][pl.ds(ks, nc), :])
            if masked:
                r = r0 + lax.broadcasted_iota(jnp.int32, (nr, nc), 0)
                c = lax.broadcasted_iota(jnp.int32, (nr, nc), 1)
                s = jnp.where(c <= r, s, NEG_BIG)
            s_ref[e, rows, 0:nc] = s
        for e in range(2):
            m_prev = m_ref[e, rows, :]
            m_new = jnp.maximum(m_prev, jnp.max(s_ref[e, rows, 0:nc], axis=-1, keepdims=True))
            m_ref[e, rows, :] = m_new
            p_ref[e, rows, 0:nc] = jnp.exp2(
                s_ref[e, rows, 0:nc] - jnp.concatenate([m_new] * (nc // LANES), axis=1)).astype(BF16)
            pv = _dot(p_ref[e, rows, 0:nc], v_refs[e][pl.ds(ks, nc), :])
            acc_ref[e, rows, :] = acc_ref[e, rows, :] * jnp.exp2(m_prev - m_new) + pv

    def body(kb, carry):
        step(kb, 0, t, t, False)
        return carry

    lax.fori_loop(0, i, body, 0)
    nr = t // DIAG_SPLIT
    for r in range(DIAG_SPLIT):
        step(i, r * nr, nr, (r + 1) * nr, True)
    acc_a = acc_ref[0]
    acc_b = acc_ref[1]
    inv_a = 1.0 / acc_a[:, den_lane[0]:den_lane[0] + 1]
    inv_b = 1.0 / acc_b[:, den_lane[1]:den_lane[1] + 1]
    o_ref[...] = jnp.where(lo, acc_a * inv_a, acc_b * inv_b).astype(o_ref.dtype)


def _attention(qkv3, fk):
    B, S, _ = qkv3.shape
    t = min(ATTN_TILE, S)
    seq = lambda: pltpu.VMEM((S, LANES), BF16)
    return pl.pallas_call(
        functools.partial(_attn_kernel, t=t),
        grid=(B, HEAD_PAIRS, S // t),
        in_specs=[
            pl.BlockSpec((None, t, LANES), lambda b, h, i: (b, i, h)),
            pl.BlockSpec((None, S, LANES), lambda b, h, i: (b, 0, HEAD_PAIRS + h)),
            pl.BlockSpec((None, S, LANES), lambda b, h, i: (b, 0, 2 * HEAD_PAIRS + h)),
            pl.BlockSpec((None, None, S, LANES), lambda b, h, i: (b, h, 0, 0)),
        ],
        out_specs=pl.BlockSpec((None, t, LANES), lambda b, h, i: (b, i, h)),
        out_shape=jax.ShapeDtypeStruct((B, S, D_ATTN), BF16),
        scratch_shapes=[
            seq(), seq(), seq(), seq(),
            pltpu.VMEM((2, t, t), F32), pltpu.VMEM((2, t, t), BF16),
            pltpu.VMEM((2, t, LANES), F32), pltpu.VMEM((2, t, LANES), F32),
        ],
        compiler_params=_cparams(("parallel", "parallel", "arbitrary")),
        name="fox_attention",
    )(qkv3, qkv3, qkv3, fk)


def _ssd_kernel(*refs, L, chunks):
    for j in range(chunks):
        _ssd_chunk(*refs, L=L, j=j)


def _ssd_chunk(xbc_ref, z_ref, small_ref, cw_ref, cb_ref, dtb_ref, alog_ref, dskip_ref, ng_ref,
               e1_ref, o_ref, xbuf_ref, state_ref, *, L, j):
    c = pl.program_id(1)
    halo = CONV_HALO
    rows = slice(j * L, (j + 1) * L)

    def carry_halo():
        xbuf_ref[0:halo, :] = xbuf_ref[L:L + halo, :]

    if j == 0:
        @pl.when(c == 0)
        def _():
            xbuf_ref[0:halo, :] = jnp.zeros((halo, CONV_CH), F32)
            state_ref[...] = jnp.zeros_like(state_ref)

        pl.when(c > 0)(carry_halo)
    else:
        carry_halo()

    xbuf_ref[halo:halo + L, :] = xbc_ref[rows, :]
    w = cw_ref[...]
    conv = cb_ref[...] + w[3:4, :] * xbuf_ref[halo:halo + L, :]
    for kk in range(CONV_WIDTH - 1):
        conv = conv + w[kk:kk + 1, :] * xbuf_ref[pl.ds(halo - (CONV_WIDTH - 1) + kk, L), :]
    u = conv * _sigmoid(conv)
    xs = u[:, :D_SSD]
    bm = u[:, D_SSD:D_SSD + SSD_GROUPS * SSD_STATE]
    cm = u[:, D_SSD + SSD_GROUPS * SSD_STATE:]

    dt = _softplus(small_ref[rows, :] + dtb_ref[...])
    a = dt * (-jnp.exp(alog_ref[...]))
    acs = _dot_01_x(_tril01(L), a)
    e1 = e1_ref[...]
    dt_e = _dot_x_01(dt, e1)
    acs_e = _dot_x_01(acs, e1)
    acs_t = acs.T
    last_e = acs_e[L - 1:L, :]
    x_dt = xs * dt_e
    x_dt_bf = x_dt.astype(BF16)
    x_end_bf = (x_dt * jnp.exp(last_e - acs_e)).astype(BF16)
    grow = jnp.exp(acs_e)
    chunk_decay = jnp.exp(last_e)

    row = lax.broadcasted_iota(jnp.int32, (L, L), 0)
    col = lax.broadcasted_iota(jnp.int32, (L, L), 1)
    causal = col <= row
    lo = lax.broadcasted_iota(jnp.int32, (1, LANES), 1) < SSD_HEAD_DIM
    pairs_per_group = SSD_HEADS // SSD_GROUPS // 2

    ys = []
    for g in range(SSD_GROUPS):
        bg = bm[:, g * SSD_STATE:(g + 1) * SSD_STATE]
        bg_bf = bg.astype(BF16)
        bg_t_bf = bg.T.astype(BF16)
        cg_bf = cm[:, g * SSD_STATE:(g + 1) * SSD_STATE].astype(BF16)
        cb = _dot_nt(cg_bf, bg_bf)
        for jj in range(pairs_per_group):
            j = g * pairs_per_group + jj
            sl = slice(j * LANES, (j + 1) * LANES)
            xp = x_dt_bf[:, sl]
            yd = []
            for e in range(2):
                h = 2 * j + e
                hl = HEAD_LANE0 + h
                seg = jnp.broadcast_to(acs[:, hl:hl + 1], (L, L)) - acs_t[hl:hl + 1, :]
                dec = jnp.exp(jnp.where(causal, seg, NEG_BIG))
                yd.append(_dot((cb * dec).astype(BF16), xp))
            st = state_ref[j]
            y_off = _dot(cg_bf, st.astype(BF16)) * grow[:, sl]
            ys.append(jnp.where(lo, yd[0], yd[1]) + y_off)
            state_ref[j] = st * chunk_decay[:, sl] + _dot(bg_t_bf, x_end_bf[:, sl])

    y = jnp.concatenate(ys, axis=1) + xs * dskip_ref[...]
    zz = z_ref[rows, :]
    y = y * (zz * _sigmoid(zz))
    o_ref[rows, :] = _rms(y, ng_ref[...]).astype(o_ref.dtype)


def _head_expand_mat():
    e1 = np.zeros((LANES, D_SSD), np.float32)
    for h in range(SSD_HEADS):
        e1[HEAD_LANE0 + h, h * SSD_HEAD_DIM:(h + 1) * SSD_HEAD_DIM] = 1.0
    return jnp.asarray(e1, BF16)


def _ssd(xbc3, z3, small3, conv_w, conv_b, dtb_row, alog_row, dskip_row, norm_g):
    B, S, _ = xbc3.shape
    L = min(SSD_CHUNK, S)
    chunks = SSD_CHUNKS_PER_STEP if S % (SSD_CHUNKS_PER_STEP * L) == 0 else 1
    rows = chunks * L
    e1 = _head_expand_mat()
    const = lambda shape: pl.BlockSpec(shape, lambda b, c: (0,) * len(shape))
    return pl.pallas_call(
        functools.partial(_ssd_kernel, L=L, chunks=chunks),
        grid=(B, S // rows),
        in_specs=[
            pl.BlockSpec((None, rows, CONV_CH), lambda b, c: (b, c, 0)),
            pl.BlockSpec((None, rows, D_SSD), lambda b, c: (b, c, 0)),
            pl.BlockSpec((None, rows, SMALL_W), lambda b, c: (b, c, 0)),
            const((CONV_WIDTH, CONV_CH)), const((1, CONV_CH)), const((1, SMALL_W)), const((1, SMALL_W)),
            const((1, D_SSD)), const((1, D_SSD)), const(e1.shape),
        ],
        out_specs=pl.BlockSpec((None, rows, D_SSD), lambda b, c: (b, c, 0)),
        out_shape=jax.ShapeDtypeStruct((B, S, D_SSD), BF16),
        scratch_shapes=[pltpu.VMEM((L + CONV_HALO, CONV_CH), F32),
                        pltpu.VMEM((SSD_HEADS // 2, SSD_STATE, LANES), F32)],
        compiler_params=_cparams(("parallel", "arbitrary")),
        name="conv_ssd",
    )(xbc3, z3, small3, conv_w, conv_b, dtb_row, alog_row, dskip_row, norm_g, e1)


def _merge_kernel(ya_ref, ys_ref, gates_ref, x_ref, wa_ref, ws_ref, wo_ref, g2_ref, wr_ref, br_ref,
                  xo_ref, h_ref, lg_ref):
    D = x_ref.shape[-1]
    pa = _dot(ya_ref[...], wa_ref[...])
    ps = _dot(ys_ref[...], ws_ref[...])
    merged = _sigmoid(gates_ref[:, :D]) * pa + _sigmoid(gates_ref[:, D:]) * ps
    xn = x_ref[...] + _dot(merged.astype(BF16), wo_ref[...])
    xo_ref[...] = xn
    h = _rms(xn, g2_ref[...])
    h_ref[...] = h.astype(BF16)
    h_hi, h_mid, _ = _split3(h)
    w_hi, w_mid, _ = _split3(wr_ref[...])
    lg_ref[...] = _dot(h_hi, w_hi) + _dot(h_hi, w_mid) + _dot(h_mid, w_hi) + br_ref[...]


def _merge(ya, ys, gates, x2, wa, ws, wo, g2, wr, br):
    T, D = x2.shape
    tm = min(ROW_TILE, T)
    row = lambda n: pl.BlockSpec((tm, n), lambda i: (i, 0))
    const = lambda a: pl.BlockSpec(a.shape, lambda i: (0, 0))
    return pl.pallas_call(
        _merge_kernel,
        grid=(T // tm,),
        in_specs=[row(D_ATTN), row(D_SSD), row(2 * D), row(D),
                  const(wa), const(ws), const(wo), const(g2), const(wr), const(br)],
        out_specs=[row(D), row(D), row(LANES)],
        out_shape=[jax.ShapeDtypeStruct((T, D), F32), jax.ShapeDtypeStruct((T, D), BF16),
                   jax.ShapeDtypeStruct((T, LANES), F32)],
        compiler_params=_cparams(("parallel",)),
        name="merge_outproj_router",
    )(ya, ys, gates, x2, wa, ws, wo, g2, wr, br)


def _route_kernel(lg_ref, route_ref, cnt_ref, carry_ref):
    i = pl.program_id(0)

    @pl.when(i == 0)
    def _():
        carry_ref[...] = jnp.zeros_like(carry_ref)

    lg = lg_ref[...]
    tm = lg.shape[0]
    lane_i = lax.broadcasted_iota(jnp.int32, (tm, LANES), 1)
    lane = lane_i.astype(F32)
    first = lambda cond: jnp.min(jnp.where(cond, lane, float(LANES)), axis=-1, keepdims=True)

    gmask = lane_i < N_EXPERT_GROUPS
    gl = jnp.where(gmask, lg, NEG_BIG)
    gmax = jnp.max(gl, axis=-1, keepdims=True)
    gsum = jnp.sum(jnp.where(gmask, jnp.exp(gl - gmax), 0.0), axis=-1, keepdims=True)
    g_idx = first(gmask & (gl == gmax))
    g_w = 1.0 / gsum

    e_lo = N_EXPERT_GROUPS + g_idx * EXPERTS_PER_GROUP
    emask = (lane >= e_lo) & (lane < e_lo + EXPERTS_PER_GROUP)
    el = jnp.where(emask, lg, NEG_BIG)
    emax = jnp.max(el, axis=-1, keepdims=True)
    esum = jnp.sum(jnp.where(emask, jnp.exp(el - emax), 0.0), axis=-1, keepdims=True)
    l0 = first(emask & (el == emax))
    el2 = jnp.where(lane == l0, NEG_BIG, el)
    m2 = jnp.max(el2, axis=-1, keepdims=True)
    l1 = first(emask & (lane != l0) & (el2 == m2))
    gate0 = g_w * (1.0 / esum)
    gate1 = g_w * (jnp.exp(m2 - emax) / esum)

    is0 = lane == l0
    is1 = lane == l1
    onehot = jnp.where(is0 | is1, 1.0, 0.0)
    before = _dot(_tril01(tm, strict=True), onehot.astype(BF16)) + carry_ref[0:1, :]
    rank0 = jnp.sum(jnp.where(is0, before, 0.0), axis=-1, keepdims=True)
    rank1 = jnp.sum(jnp.where(is1, before, 0.0), axis=-1, keepdims=True)
    total = carry_ref[0:1, :] + jnp.sum(onehot, axis=0, keepdims=True)
    carry_ref[0:1, :] = total
    cnt_ref[...] = jnp.broadcast_to(total, cnt_ref.shape)

    rec = jnp.zeros((tm, LANES), F32)
    for ln, val in ((R_E0, l0 - N_EXPERT_GROUPS), (R_E1, l1 - N_EXPERT_GROUPS), (R_G0, gate0), (R_G1, gate1),
                    (R_RANK0, rank0), (R_RANK1, rank1)):
        rec = jnp.where(lane_i == ln, val, rec)
    route_ref[...] = rec


def _route(logits):
    T = logits.shape[0]
    tm = min(ROW_TILE, T)
    return pl.pallas_call(
        _route_kernel,
        grid=(T // tm,),
        in_specs=[pl.BlockSpec((tm, LANES), lambda i: (i, 0))],
        out_specs=[pl.BlockSpec((tm, LANES), lambda i: (i, 0)), pl.BlockSpec((8, LANES), lambda i: (0, 0))],
        out_shape=[jax.ShapeDtypeStruct((T, LANES), F32), jax.ShapeDtypeStruct((8, LANES), F32)],
        scratch_shapes=[pltpu.VMEM((8, LANES), F32)],
        compiler_params=_cparams(("arbitrary",)),
        name="route",
    )(logits)


def _table_lookup(idx, table):
    n = table.shape[0]
    return jnp.sum(jnp.where(idx[:, None] == jnp.arange(n, dtype=idx.dtype)[None, :], table[None, :], 0), axis=1)


def _dispatch_plan(route, cnt):
    T = route.shape[0]
    counts = cnt[0, N_EXPERT_GROUPS:N_EXPERT_GROUPS + N_EXPERTS].astype(jnp.int32)
    padded = (counts + MOE_BLOCK - 1) // MOE_BLOCK * MOE_BLOCK
    pad_end = jnp.cumsum(padded)
    pad_start = pad_end - padded
    start = jnp.cumsum(counts) - counts
    e0 = route[:, R_E0].astype(jnp.int32)
    e1 = route[:, R_E1].astype(jnp.int32)
    dest0 = _table_lookup(e0, pad_start) + route[:, R_RANK0].astype(jnp.int32)
    dest1 = _table_lookup(e1, pad_start) + route[:, R_RANK1].astype(jnp.int32)
    n_blocks = -(-(T * TOP_K) // MOE_BLOCK) + N_EXPERTS
    block_end = pad_end // MOE_BLOCK
    blk = jnp.arange(n_blocks, dtype=jnp.int32)
    block_expert = jnp.minimum(jnp.sum(blk[:, None] >= block_end[None, :], axis=1), N_EXPERTS - 1).astype(jnp.int32)
    n_used = block_end[-1:].astype(jnp.int32)
    order = jnp.argsort(jnp.stack([e0, e1], axis=1).reshape(-1), stable=True).astype(jnp.int32)
    shift = _table_lookup(block_expert, start - pad_start)
    pos = jnp.arange(n_blocks * MOE_BLOCK, dtype=jnp.int32) + jnp.repeat(shift, MOE_BLOCK)
    src_tok = order[jnp.clip(pos, 0, T * TOP_K - 1)] // TOP_K
    return dest0, dest1, src_tok, block_expert, n_used


def _expert_kernel(be_ref, nu_ref, x_ref, w1_ref, w3_ref, w2_ref, o_ref, w1b_ref, w3b_ref, w2b_ref):
    i = pl.program_id(0)
    last = nu_ref[0] - 1
    cur = be_ref[jnp.minimum(i, last)]
    prev = be_ref[jnp.minimum(jnp.maximum(i - 1, 0), last)]

    @pl.when((i == 0) | (cur != prev))
    def _():
        w1b_ref[...] = w1_ref[...].astype(BF16)
        w3b_ref[...] = w3_ref[...].astype(BF16)
        w2b_ref[...] = w2_ref[...].astype(BF16)

    @pl.when(i <= last)
    def _():
        x = x_ref[...]
        h1 = _dot(x, w1b_ref[...])
        h3 = _dot(x, w3b_ref[...])
        a = (h1 * _sigmoid(h1) * h3).astype(BF16)
        o_ref[...] = _dot(a, w2b_ref[...])


def _experts(buf, block_expert, n_used, w1, w3, w2, layer):
    R, D = buf.shape
    FF = w1.shape[-1]
    nb = R // MOE_BLOCK
    blk = lambda i, be, nu: (jnp.minimum(i, nu[0] - 1), 0)
    wsel = lambda i, be, nu: (layer, be[jnp.minimum(i, nu[0] - 1)], 0, 0)
    return pl.pallas_call(
        _expert_kernel,
        grid_spec=pltpu.PrefetchScalarGridSpec(
            num_scalar_prefetch=2,
            grid=(nb,),
            in_specs=[
                pl.BlockSpec((MOE_BLOCK, D), blk),
                pl.BlockSpec((None, None, D, FF), wsel),
                pl.BlockSpec((None, None, D, FF), wsel),
                pl.BlockSpec((None, None, FF, D), wsel),
            ],
            out_specs=pl.BlockSpec((MOE_BLOCK, D), blk),
            scratch_shapes=[pltpu.VMEM((D, FF), BF16), pltpu.VMEM((D, FF), BF16), pltpu.VMEM((FF, D), BF16)],
        ),
        out_shape=jax.ShapeDtypeStruct((R, D), F32),
        compiler_params=_cparams(("arbitrary",)),
        name="expert_mlp",
    )(block_expert, n_used, buf, w1, w3, w2)


def _final_kernel(x_ref, r0_ref, r1_ref, rt_ref, g_ref, o_ref):
    o_ref[...] = _rms(_moe_combine(x_ref, r0_ref, r1_ref, rt_ref), g_ref[...])


def _final_norm(x2, moe, g):
    T, D = x2.shape
    tm = min(ROW_TILE, T)
    row = lambda n: pl.BlockSpec((tm, n), lambda i: (i, 0))
    return pl.pallas_call(
        _final_kernel,
        grid=(T // tm,),
        in_specs=[row(D), row(D), row(D), row(LANES), pl.BlockSpec((1, D), lambda i: (0, 0))],
        out_specs=row(D),
        out_shape=jax.ShapeDtypeStruct((T, D), F32),
        compiler_params=_cparams(("parallel",)),
        name="final_norm",
    )(x2, *moe, g)


def _pack_plan(D):
    widths = (D_ATTN, D_ATTN, D_ATTN, ATTN_HEADS, D_SSD, CONV_CH, SSD_HEADS, D, D)
    src = dict(zip(("q", "k", "v", "f", "z", "xbc", "dt", "ga", "gs"),
                   zip(np.concatenate([[0], np.cumsum(widths)[:-1]]).tolist(), widths)))
    plan, dst = [], 0
    for name in ("q", "k", "v", "z", "xbc", "ga", "gs", "f", "dt"):
        s0, w = src[name]
        plan.append((s0, w, dst, ATTN_HEAD_DIM ** -0.5 * LOG2E if name == "q" else 1.0))
        dst += w
    return plan, dst


def _pack_kernel(w_ref, o_ref, *, plan, used):
    for s0, w, d0, scale in plan:
        piece = w_ref[:, s0:s0 + w]
        if scale != 1.0:
            piece = piece * scale
        o_ref[:, d0:d0 + w] = piece.astype(BF16)
    o_ref[:, used:] = jnp.zeros((o_ref.shape[0], o_ref.shape[1] - used), BF16)


def _pack_w_in(w_in):
    depth, D, cols = w_in.shape
    plan, used = _pack_plan(D)
    n_out = -(-used // LANES) * LANES
    rows = PACK_ROWS
    return pl.pallas_call(
        functools.partial(_pack_kernel, plan=plan, used=used),
        grid=(depth, D // rows),
        in_specs=[pl.BlockSpec((None, rows, cols), lambda l, r: (l, r, 0))],
        out_specs=pl.BlockSpec((None, rows, n_out), lambda l, r: (l, r, 0)),
        out_shape=jax.ShapeDtypeStruct((depth, D, n_out), BF16),
        compiler_params=_cparams(("parallel", "parallel")),
        name="pack_w_in",
    )(w_in)


def _small_rows(mat, lane0):
    depth, n = mat.shape
    return jnp.zeros((depth, 1, SMALL_W), F32).at[:, 0, lane0:lane0 + n].set(mat.astype(F32))


def kernel(x, norm_mix_g, w_in, b_f, conv_w, conv_b, dt_bias, a_log, d_skip, ssd_norm_g, w_br_attn, w_br_ssd,
           w_out, norm_ffn_g, w_group_router, b_group_router, w_expert_router, b_expert_router, w1, w3, w2,
           final_g):
    B, S, D = x.shape
    T = B * S
    depth = w_in.shape[0]
    w_packed = _pack_w_in(w_in)
    bf_rows = _small_rows(b_f, 0)
    dtb_rows = _small_rows(dt_bias, HEAD_LANE0)
    alog_rows = _small_rows(a_log, HEAD_LANE0)
    dskip_rows = jnp.repeat(d_skip, SSD_HEAD_DIM, axis=1)[:, None, :]
    router_pad = LANES - N_EXPERT_GROUPS - N_EXPERTS
    w_router = jnp.concatenate([w_group_router, w_expert_router, jnp.zeros((depth, D, router_pad), F32)], axis=2)
    b_router = jnp.concatenate([b_group_router, b_expert_router, jnp.zeros((depth, router_pad), F32)],
                               axis=1)[:, None, :]
    wa, ws, wo = w_br_attn.astype(BF16), w_br_ssd.astype(BF16), w_out.astype(BF16)

    x2 = x.reshape(T, D)
    moe = None
    for l in range(depth):
        x2, (qkv, z, xbc, gates, small) = _inproj(x2, moe, norm_mix_g[l][None, :], w_packed, l)
        small3 = small.reshape(B, S, SMALL_W)
        fk = _fcum(small3, bf_rows[l])
        y_attn = _attention(qkv.reshape(B, S, 3 * D_ATTN), fk).reshape(T, D_ATTN)
        y_ssd = _ssd(xbc.reshape(B, S, CONV_CH), z.reshape(B, S, D_SSD), small3, conv_w[l], conv_b[l][None, :],
                     dtb_rows[l], alog_rows[l], dskip_rows[l], ssd_norm_g[l][None, :]).reshape(T, D_SSD)
        x2, h2, logits = _merge(y_attn, y_ssd, gates, x2, wa[l], ws[l], wo[l], norm_ffn_g[l][None, :],
                                w_router[l], b_router[l])
        route, cnt = _route(logits)
        dest0, dest1, src_tok, block_expert, n_used = _dispatch_plan(route, cnt)
        out = _experts(h2[src_tok], block_expert, n_used, w1, w3, w2, l)
        moe = (out[dest0], out[dest1], route)
    return _final_norm(x2, moe, final_g[None, :]).reshape(B, S, D)
```

```python
import functools

import numpy as np
import jax
import jax.numpy as jnp
from jax import lax
from jax.experimental import pallas as pl
from jax.experimental.pallas import tpu as pltpu

F32 = jnp.float32
BF16 = jnp.bfloat16

ATTN_HEADS = 8
ATTN_HEAD_DIM = 64
D_ATTN = ATTN_HEADS * ATTN_HEAD_DIM
SSD_HEADS = 16
SSD_HEAD_DIM = 64
D_SSD = SSD_HEADS * SSD_HEAD_DIM
SSD_GROUPS = 2
SSD_STATE = 128
CONV_WIDTH = 4
CONV_CH = D_SSD + 2 * SSD_GROUPS * SSD_STATE
N_EXPERT_GROUPS = 4
EXPERTS_PER_GROUP = 8
N_EXPERTS = N_EXPERT_GROUPS * EXPERTS_PER_GROUP
TOP_K = 2
EPS = 1e-6

LANES = 128
HEAD_LANE0 = 8
SMALL_W = LANES
NEG_BIG = -1e30
LOG2E = 1.4426950408889634
VMEM_LIMIT = 56 * 1024 * 1024

SSD_CHUNK = 128
SSD_CHUNKS_PER_STEP = 4
CONV_HALO = 8
ATTN_TILE = 1024
DIAG_SPLIT = 2
FCUM_TILE = 512
ROW_TILE = 512
PROJ_COLS = 512
PACK_ROWS = 128
MOE_BLOCK = 256
R_E0, R_E1, R_G0, R_G1, R_RANK0, R_RANK1 = range(6)


def _cparams(sem):
    return pltpu.CompilerParams(dimension_semantics=sem, vmem_limit_bytes=VMEM_LIMIT)


def _split3(x):
    hi = x.astype(BF16)
    r = x - hi.astype(F32)
    mid = r.astype(BF16)
    lo = (r - mid.astype(F32)).astype(BF16)
    return hi, mid, lo


def _dot(a, b):
    return jnp.dot(a, b, preferred_element_type=F32)


def _dot_nt(a, b):
    return lax.dot_general(a, b, (((1,), (1,)), ((), ())), preferred_element_type=F32)


def _dot_x_01(x, m01):
    hi, mid, lo = _split3(x)
    return _dot(hi, m01) + _dot(mid, m01) + _dot(lo, m01)


def _dot_01_x(m01, x):
    hi, mid, lo = _split3(x)
    return _dot(m01, hi) + _dot(m01, mid) + _dot(m01, lo)


def _tril01(n, strict=False):
    r = lax.broadcasted_iota(jnp.int32, (n, n), 0)
    c = lax.broadcasted_iota(jnp.int32, (n, n), 1)
    return ((c < r) if strict else (c <= r)).astype(BF16)


def _softplus(x):
    return jnp.maximum(x, 0.0) + jnp.log1p(jnp.exp(-jnp.abs(x)))


def _sigmoid(x):
    return 0.5 * jnp.tanh(0.5 * x) + 0.5


def _rms(x, g):
    ms = jnp.mean(x * x, axis=-1, keepdims=True)
    return x * lax.rsqrt(ms + EPS) * g


def _moe_combine(x_ref, r0_ref, r1_ref, rt_ref):
    rt = rt_ref[...]
    return x_ref[...] + (rt[:, R_G0:R_G0 + 1] * r0_ref[...] + rt[:, R_G1:R_G1 + 1] * r1_ref[...])


def _inproj_kernel(*refs, with_moe):
    if with_moe:
        x_ref, r0_ref, r1_ref, rt_ref, g_ref, w_ref, xo_ref, *outs = refs
        x = _moe_combine(x_ref, r0_ref, r1_ref, rt_ref)
        xo_ref[...] = x
    else:
        x_ref, g_ref, w_ref, *outs = refs
        x = x_ref[...]
    h = _rms(x, g_ref[...]).astype(BF16)
    off = 0
    for ref in outs:
        n = ref.shape[-1]
        for c0 in range(0, n, PROJ_COLS):
            c1 = min(c0 + PROJ_COLS, n)
            ref[:, c0:c1] = _dot(h, w_ref[:, off + c0:off + c1]).astype(ref.dtype)
        off += n


def _inproj(x2, moe, g, w_packed, layer):
    T, D = x2.shape
    widths = (3 * D_ATTN, D_SSD, CONV_CH, 2 * D, SMALL_W)
    dtypes = (BF16, F32, F32, F32, F32)
    tm = min(ROW_TILE, T)
    row = lambda n: pl.BlockSpec((tm, n), lambda i: (i, 0))
    with_moe = moe is not None
    ins = [x2] + (list(moe) if with_moe else []) + [g, w_packed]
    in_specs = [row(D)] + ([row(D), row(D), row(LANES)] if with_moe else []) + [
        pl.BlockSpec((1, D), lambda i: (0, 0)),
        pl.BlockSpec((None,) + w_packed.shape[1:], lambda i: (layer, 0, 0), pipeline_mode=pl.Buffered(1)),
    ]
    out_specs = ([row(D)] if with_moe else []) + [row(n) for n in widths]
    out_shape = ([jax.ShapeDtypeStruct((T, D), F32)] if with_moe else []) + [
        jax.ShapeDtypeStruct((T, n), dt) for n, dt in zip(widths, dtypes)]
    res = pl.pallas_call(
        functools.partial(_inproj_kernel, with_moe=with_moe),
        grid=(T // tm,),
        in_specs=in_specs,
        out_specs=out_specs,
        out_shape=out_shape,
        compiler_params=_cparams(("parallel",)),
        name="inproj",
    )(*ins)
    return (res[0], res[1:]) if with_moe else (x2, res)


N_FPARTS = 3
HEAD_PAIRS = D_ATTN // LANES


def _fcum_kernel(s_ref, b_ref, place_ref, fk_ref, carry_ref):
    c = pl.program_id(1)

    @pl.when(c == 0)
    def _():
        carry_ref[...] = jnp.zeros_like(carry_ref)

    x = s_ref[...] + b_ref[...]
    logf = jnp.minimum(x, 0.0) - jnp.log1p(jnp.exp(-jnp.abs(x)))
    n = x.shape[0]
    cum = _dot_01_x(_tril01(n), logf) + carry_ref[0:1, :]
    carry_ref[0:1, :] = cum[n - 1:n, :]
    parts = jnp.concatenate(_split3(cum * (-LOG2E)), axis=1)
    for p in range(HEAD_PAIRS):
        fk_ref[p] = _dot(parts, place_ref[p]).astype(BF16)


def _fpart_placement():
    m = np.zeros((HEAD_PAIRS, N_FPARTS * LANES, LANES), np.float32)
    for p in range(HEAD_PAIRS):
        for j in range(N_FPARTS):
            m[p, j * LANES + 2 * p, ATTN_HEAD_DIM + j] = 1.0
            m[p, j * LANES + 2 * p + 1, j] = 1.0
    return jnp.asarray(m, BF16)


def _fcum(small3, bf_row):
    B, S, _ = small3.shape
    L = min(FCUM_TILE, S)
    place = _fpart_placement()
    return pl.pallas_call(
        _fcum_kernel,
        grid=(B, S // L),
        in_specs=[
            pl.BlockSpec((None, L, SMALL_W), lambda b, c: (b, c, 0)),
            pl.BlockSpec((1, SMALL_W), lambda b, c: (0, 0)),
            pl.BlockSpec(place.shape, lambda b, c: (0, 0, 0)),
        ],
        out_specs=pl.BlockSpec((None, HEAD_PAIRS, L, LANES), lambda b, c: (b, 0, c, 0)),
        out_shape=jax.ShapeDtypeStruct((B, HEAD_PAIRS, S, LANES), BF16),
        scratch_shapes=[pltpu.VMEM((8, SMALL_W), F32)],
        compiler_params=_cparams(("parallel", "arbitrary")),
        name="forget_cumsum",
    )(small3, bf_row, place)


def _attn_kernel(q_ref, k_ref, v_ref, fk_ref, o_ref, ka_ref, kb_ref, va_ref, vb_ref, s_ref, p_ref, m_ref,
                 acc_ref, *, t):
    i = pl.program_id(2)
    lane = lax.broadcasted_iota(jnp.int32, (1, LANES), 1)
    lo = lane < ATTN_HEAD_DIM
    den_lane = (ATTN_HEAD_DIM, 0)
    ones_at = lambda cond: jnp.where(cond, 1.0, 0.0).astype(BF16)

    @pl.when(i == 0)
    def _():
        k = k_ref[...]
        fk = fk_ref[...]
        ka_ref[...] = jnp.where(lo, k, fk)
        kb_ref[...] = jnp.where(lo, fk, k)
        v = v_ref[...]
        va_ref[...] = jnp.where(lo, v, ones_at(lane == den_lane[0]))
        vb_ref[...] = jnp.where(lo, ones_at(lane == den_lane[1]), v)

    q = q_ref[...]
    bias_a = ones_at((lane >= ATTN_HEAD_DIM) & (lane < ATTN_HEAD_DIM + N_FPARTS))
    bias_b = ones_at(lane < N_FPARTS)
    q_heads = (jnp.where(lo, q, bias_a), jnp.where(lo, bias_b, q))
    k_refs = (ka_ref, kb_ref)
    v_refs = (va_ref, vb_ref)
    m_ref[...] = jnp.full(m_ref.shape, NEG_BIG, F32)
    acc_ref[...] = jnp.zeros(acc_ref.shape, F32)

    def step(kb, r0, nr, nc, masked):
        ks = pl.multiple_of(kb * t, t)
        rows = slice(r0, r0 + nr)
        for e in range(2):
            s = _dot_nt(q_heads[e][rows], k_refs[e][pl.ds(ks, nc), :])
            if masked:
                r = r0 + lax.broadcasted_iota(jnp.int32, (nr, nc), 0)
                c = lax.broadcasted_iota(jnp.int32, (nr, nc), 1)
                s = jnp.where(c <= r, s, NEG_BIG)
            s_ref[e, rows, 0:nc] = s
        for e in range(2):
            m_prev = m_ref[e, rows, :]
            m_new = jnp.maximum(m_prev, jnp.max(s_ref[e, rows, 0:nc], axis=-1, keepdims=True))
            m_ref[e, rows, :] = m_new
            p_ref[e, rows, 0:nc] = jnp.exp2(
                s_ref[e, rows, 0:nc] - jnp.concatenate([m_new] * (nc // LANES), axis=1)).astype(BF16)
            pv = _dot(p_ref[e, rows, 0:nc], v_refs[e][pl.ds(ks, nc), :])
            acc_ref[e, rows, :] = acc_ref[e, rows, :] * jnp.exp2(m_prev - m_new) + pv

    def body(kb, carry):
        step(kb, 0, t, t, False)
        return carry

    lax.fori_loop(0, i, body, 0)
    nr = t // DIAG_SPLIT
    for r in range(DIAG_SPLIT):
        step(i, r * nr, nr, (r + 1) * nr, True)
    acc_a = acc_ref[0]
    acc_b = acc_ref[1]
    inv_a = 1.0 / acc_a[:, den_lane[0]:den_lane[0] + 1]
    inv_b = 1.0 / acc_b[:, den_lane[1]:den_lane[1] + 1]
    o_ref[...] = jnp.where(lo, acc_a * inv_a, acc_b * inv_b).astype(o_ref.dtype)


def _attention(qkv3, fk):
    B, S, _ = qkv3.shape
    t = min(ATTN_TILE, S)
    seq = lambda: pltpu.VMEM((S, LANES), BF16)
    return pl.pallas_call(
        functools.partial(_attn_kernel, t=t),
        grid=(B, HEAD_PAIRS, S // t),
        in_specs=[
            pl.BlockSpec((None, t, LANES), lambda b, h, i: (b, i, h)),
            pl.BlockSpec((None, S, LANES), lambda b, h, i: (b, 0, HEAD_PAIRS + h)),
            pl.BlockSpec((None, S, LANES), lambda b, h, i: (b, 0, 2 * HEAD_PAIRS + h)),
            pl.BlockSpec((None, None, S, LANES), lambda b, h, i: (b, h, 0, 0)),
        ],
        out_specs=pl.BlockSpec((None, t, LANES), lambda b, h, i: (b, i, h)),
        out_shape=jax.ShapeDtypeStruct((B, S, D_ATTN), BF16),
        scratch_shapes=[
            seq(), seq(), seq(), seq(),
            pltpu.VMEM((2, t, t), F32), pltpu.VMEM((2, t, t), BF16),
            pltpu.VMEM((2, t, LANES), F32), pltpu.VMEM((2, t, LANES), F32),
        ],
        compiler_params=_cparams(("parallel", "parallel", "arbitrary")),
        name="fox_attention",
    )(qkv3, qkv3, qkv3, fk)


def _ssd_kernel(*refs, L, chunks):
    for j in range(chunks):
        _ssd_chunk(*refs, L=L, j=j)


def _ssd_chunk(xbc_ref, z_ref, small_ref, cw_ref, cb_ref, dtb_ref, alog_ref, dskip_ref, ng_ref,
               e1_ref, o_ref, xbuf_ref, state_ref, *, L, j):
    c = pl.program_id(1)
    halo = CONV_HALO
    rows = slice(j * L, (j + 1) * L)

    def carry_halo():
        xbuf_ref[0:halo, :] = xbuf_ref[L:L + halo, :]

    if j == 0:
        @pl.when(c == 0)
        def _():
            xbuf_ref[0:halo, :] = jnp.zeros((halo, CONV_CH), F32)
            state_ref[...] = jnp.zeros_like(state_ref)

        pl.when(c > 0)(carry_halo)
    else:
        carry_halo()

    xbuf_ref[halo:halo + L, :] = xbc_ref[rows, :]
    w = cw_ref[...]
    conv = cb_ref[...] + w[3:4, :] * xbuf_ref[halo:halo + L, :]
    for kk in range(CONV_WIDTH - 1):
        conv = conv + w[kk:kk + 1, :] * xbuf_ref[pl.ds(halo - (CONV_WIDTH - 1) + kk, L), :]
    u = conv * _sigmoid(conv)
    xs = u[:, :D_SSD]
    bm = u[:, D_SSD:D_SSD + SSD_GROUPS * SSD_STATE]
    cm = u[:, D_SSD + SSD_GROUPS * SSD_STATE:]

    dt = _softplus(small_ref[rows, :] + dtb_ref[...])
    a = dt * (-jnp.exp(alog_ref[...]))
    acs = _dot_01_x(_tril01(L), a)
    e1 = e1_ref[...]
    dt_e = _dot_x_01(dt, e1)
    acs_e = _dot_x_01(acs, e1)
    acs_t = acs.T
    last_e = acs_e[L - 1:L, :]
    x_dt = xs * dt_e
    x_dt_bf = x_dt.astype(BF16)
    x_end_bf = (x_dt * jnp.exp(last_e - acs_e)).astype(BF16)
    grow = jnp.exp(acs_e)
    chunk_decay = jnp.exp(last_e)

    row = lax.broadcasted_iota(jnp.int32, (L, L), 0)
    col = lax.broadcasted_iota(jnp.int32, (L, L), 1)
    causal = col <= row
    lo = lax.broadcasted_iota(jnp.int32, (1, LANES), 1) < SSD_HEAD_DIM
    pairs_per_group = SSD_HEADS // SSD_GROUPS // 2

    ys = []
    for g in range(SSD_GROUPS):
        bg = bm[:, g * SSD_STATE:(g + 1) * SSD_STATE]
        bg_bf = bg.astype(BF16)
        bg_t_bf = bg.T.astype(BF16)
        cg_bf = cm[:, g * SSD_STATE:(g + 1) * SSD_STATE].astype(BF16)
        cb = _dot_nt(cg_bf, bg_bf)
        for jj in range(pairs_per_group):
            j = g * pairs_per_group + jj
            sl = slice(j * LANES, (j + 1) * LANES)
            xp = x_dt_bf[:, sl]
            yd = []
            for e in range(2):
                h = 2 * j + e
                hl = HEAD_LANE0 + h
                seg = jnp.broadcast_to(acs[:, hl:hl + 1], (L, L)) - acs_t[hl:hl + 1, :]
                dec = jnp.exp(jnp.where(causal, seg, NEG_BIG))
                yd.append(_dot((cb * dec).astype(BF16), xp))
            st = state_ref[j]
            y_off = _dot(cg_bf, st.astype(BF16)) * grow[:, sl]
            ys.append(jnp.where(lo, yd[0], yd[1]) + y_off)
            state_ref[j] = st * chunk_decay[:, sl] + _dot(bg_t_bf, x_end_bf[:, sl])

    y = jnp.concatenate(ys, axis=1) + xs * dskip_ref[...]
    zz = z_ref[rows, :]
    y = y * (zz * _sigmoid(zz))
    o_ref[rows, :] = _rms(y, ng_ref[...]).astype(o_ref.dtype)


def _head_expand_mat():
    e1 = np.zeros((LANES, D_SSD), np.float32)
    for h in range(SSD_HEADS):
        e1[HEAD_LANE0 + h, h * SSD_HEAD_DIM:(h + 1) * SSD_HEAD_DIM] = 1.0
    return jnp.asarray(e1, BF16)


def _ssd(xbc3, z3, small3, conv_w, conv_b, dtb_row, alog_row, dskip_row, norm_g):
    B, S, _ = xbc3.shape
    L = min(SSD_CHUNK, S)
    chunks = SSD_CHUNKS_PER_STEP if S % (SSD_CHUNKS_PER_STEP * L) == 0 else 1
    rows = chunks * L
    e1 = _head_expand_mat()
    const = lambda shape: pl.BlockSpec(shape, lambda b, c: (0,) * len(shape))
    return pl.pallas_call(
        functools.partial(_ssd_kernel, L=L, chunks=chunks),
        grid=(B, S // rows),
        in_specs=[
            pl.BlockSpec((None, rows, CONV_CH), lambda b, c: (b, c, 0)),
            pl.BlockSpec((None, rows, D_SSD), lambda b, c: (b, c, 0)),
            pl.BlockSpec((None, rows, SMALL_W), lambda b, c: (b, c, 0)),
            const((CONV_WIDTH, CONV_CH)), const((1, CONV_CH)), const((1, SMALL_W)), const((1, SMALL_W)),
            const((1, D_SSD)), const((1, D_SSD)), const(e1.shape),
        ],
        out_specs=pl.BlockSpec((None, rows, D_SSD), lambda b, c: (b, c, 0)),
        out_shape=jax.ShapeDtypeStruct((B, S, D_SSD), BF16),
        scratch_shapes=[pltpu.VMEM((L + CONV_HALO, CONV_CH), F32),
                        pltpu.VMEM((SSD_HEADS // 2, SSD_STATE, LANES), F32)],
        compiler_params=_cparams(("parallel", "arbitrary")),
        name="conv_ssd",
    )(xbc3, z3, small3, conv_w, conv_b, dtb_row, alog_row, dskip_row, norm_g, e1)


def _merge_kernel(ya_ref, ys_ref, gates_ref, x_ref, wa_ref, ws_ref, wo_ref, g2_ref, wr_ref, br_ref,
                  xo_ref, h_ref, lg_ref):
    D = x_ref.shape[-1]
    pa = _dot(ya_ref[...], wa_ref[...])
    ps = _dot(ys_ref[...], ws_ref[...])
    merged = _sigmoid(gates_ref[:, :D]) * pa + _sigmoid(gates_ref[:, D:]) * ps
    xn = x_ref[...] + _dot(merged.astype(BF16), wo_ref[...])
    xo_ref[...] = xn
    h = _rms(xn, g2_ref[...])
    h_ref[...] = h.astype(BF16)
    h_hi, h_mid, _ = _split3(h)
    w_hi, w_mid, _ = _split3(wr_ref[...])
    lg_ref[...] = _dot(h_hi, w_hi) + _dot(h_hi, w_mid) + _dot(h_mid, w_hi) + br_ref[...]


def _merge(ya, ys, gates, x2, wa, ws, wo, g2, wr, br):
    T, D = x2.shape
    tm = min(ROW_TILE, T)
    row = lambda n: pl.BlockSpec((tm, n), lambda i: (i, 0))
    const = lambda a: pl.BlockSpec(a.shape, lambda i: (0, 0))
    return pl.pallas_call(
        _merge_kernel,
        grid=(T // tm,),
        in_specs=[row(D_ATTN), row(D_SSD), row(2 * D), row(D),
                  const(wa), const(ws), const(wo), const(g2), const(wr), const(br)],
        out_specs=[row(D), row(D), row(LANES)],
        out_shape=[jax.ShapeDtypeStruct((T, D), F32), jax.ShapeDtypeStruct((T, D), BF16),
                   jax.ShapeDtypeStruct((T, LANES), F32)],
        compiler_params=_cparams(("parallel",)),
        name="merge_outproj_router",
    )(ya, ys, gates, x2, wa, ws, wo, g2, wr, br)


def _route_kernel(lg_ref, route_ref, cnt_ref, carry_ref):
    i = pl.program_id(0)

    @pl.when(i == 0)
    def _():
        carry_ref[...] = jnp.zeros_like(carry_ref)

    lg = lg_ref[...]
    tm = lg.shape[0]
    lane_i = lax.broadcasted_iota(jnp.int32, (tm, LANES), 1)
    lane = lane_i.astype(F32)
    first = lambda cond: jnp.min(jnp.where(cond, lane, float(LANES)), axis=-1, keepdims=True)

    gmask = lane_i < N_EXPERT_GROUPS
    gl = jnp.where(gmask, lg, NEG_BIG)
    gmax = jnp.max(gl, axis=-1, keepdims=True)
    gsum = jnp.sum(jnp.where(gmask, jnp.exp(gl - gmax), 0.0), axis=-1, keepdims=True)
    g_idx = first(gmask & (gl == gmax))
    g_w = 1.0 / gsum

    e_lo = N_EXPERT_GROUPS + g_idx * EXPERTS_PER_GROUP
    emask = (lane >= e_lo) & (lane < e_lo + EXPERTS_PER_GROUP)
    el = jnp.where(emask, lg, NEG_BIG)
    emax = jnp.max(el, axis=-1, keepdims=True)
    esum = jnp.sum(jnp.where(emask, jnp.exp(el - emax), 0.0), axis=-1, keepdims=True)
    l0 = first(emask & (el == emax))
    el2 = jnp.where(lane == l0, NEG_BIG, el)
    m2 = jnp.max(el2, axis=-1, keepdims=True)
    l1 = first(emask & (lane != l0) & (el2 == m2))
    gate0 = g_w * (1.0 / esum)
    gate1 = g_w * (jnp.exp(m2 - emax) / esum)

    is0 = lane == l0
    is1 = lane == l1
    onehot = jnp.where(is0 | is1, 1.0, 0.0)
    before = _dot(_tril01(tm, strict=True), onehot.astype(BF16)) + carry_ref[0:1, :]
    rank0 = jnp.sum(jnp.where(is0, before, 0.0), axis=-1, keepdims=True)
    rank1 = jnp.sum(jnp.where(is1, before, 0.0), axis=-1, keepdims=True)
    total = carry_ref[0:1, :] + jnp.sum(onehot, axis=0, keepdims=True)
    carry_ref[0:1, :] = total
    cnt_ref[...] = jnp.broadcast_to(total, cnt_ref.shape)

    rec = jnp.zeros((tm, LANES), F32)
    for ln, val in ((R_E0, l0 - N_EXPERT_GROUPS), (R_E1, l1 - N_EXPERT_GROUPS), (R_G0, gate0), (R_G1, gate1),
                    (R_RANK0, rank0), (R_RANK1, rank1)):
        rec = jnp.where(lane_i == ln, val, rec)
    route_ref[...] = rec


def _route(logits):
    T = logits.shape[0]
    tm = min(ROW_TILE, T)
    return pl.pallas_call(
        _route_kernel,
        grid=(T // tm,),
        in_specs=[pl.BlockSpec((tm, LANES), lambda i: (i, 0))],
        out_specs=[pl.BlockSpec((tm, LANES), lambda i: (i, 0)), pl.BlockSpec((8, LANES), lambda i: (0, 0))],
        out_shape=[jax.ShapeDtypeStruct((T, LANES), F32), jax.ShapeDtypeStruct((8, LANES), F32)],
        scratch_shapes=[pltpu.VMEM((8, LANES), F32)],
        compiler_params=_cparams(("arbitrary",)),
        name="route",
    )(logits)


def _table_lookup(idx, table):
    n = table.shape[0]
    return jnp.sum(jnp.where(idx[:, None] == jnp.arange(n, dtype=idx.dtype)[None, :], table[None, :], 0), axis=1)


def _dispatch_plan(route, cnt):
    T = route.shape[0]
    counts = cnt[0, N_EXPERT_GROUPS:N_EXPERT_GROUPS + N_EXPERTS].astype(jnp.int32)
    padded = (counts + MOE_BLOCK - 1) // MOE_BLOCK * MOE_BLOCK
    pad_end = jnp.cumsum(padded)
    pad_start = pad_end - padded
    start = jnp.cumsum(counts) - counts
    e0 = route[:, R_E0].astype(jnp.int32)
    e1 = route[:, R_E1].astype(jnp.int32)
    dest0 = _table_lookup(e0, pad_start) + route[:, R_RANK0].astype(jnp.int32)
    dest1 = _table_lookup(e1, pad_start) + route[:, R_RANK1].astype(jnp.int32)
    n_blocks = -(-(T * TOP_K) // MOE_BLOCK) + N_EXPERTS
    block_end = pad_end // MOE_BLOCK
    blk = jnp.arange(n_blocks, dtype=jnp.int32)
    block_expert = jnp.minimum(jnp.sum(blk[:, None] >= block_end[None, :], axis=1), N_EXPERTS - 1).astype(jnp.int32)
    n_used = block_end[-1:].astype(jnp.int32)
    order = jnp.argsort(jnp.stack([e0, e1], axis=1).reshape(-1), stable=True).astype(jnp.int32)
    shift = _table_lookup(block_expert, start - pad_start)
    pos = jnp.arange(n_blocks * MOE_BLOCK, dtype=jnp.int32) + jnp.repeat(shift, MOE_BLOCK)
    src_tok = order[jnp.clip(pos, 0, T * TOP_K - 1)] // TOP_K
    return dest0, dest1, src_tok, block_expert, n_used


def _expert_kernel(be_ref, nu_ref, x_ref, w1_ref, w3_ref, w2_ref, o_ref, w1b_ref, w3b_ref, w2b_ref):
    i = pl.program_id(0)
    last = nu_ref[0] - 1
    cur = be_ref[jnp.minimum(i, last)]
    prev = be_ref[jnp.minimum(jnp.maximum(i - 1, 0), last)]

    @pl.when((i == 0) | (cur != prev))
    def _():
        w1b_ref[...] = w1_ref[...].astype(BF16)
        w3b_ref[...] = w3_ref[...].astype(BF16)
        w2b_ref[...] = w2_ref[...].astype(BF16)

    @pl.when(i <= last)
    def _():
        x = x_ref[...]
        h1 = _dot(x, w1b_ref[...])
        h3 = _dot(x, w3b_ref[...])
        a = (h1 * _sigmoid(h1) * h3).astype(BF16)
        o_ref[...] = _dot(a, w2b_ref[...])

    @pl.when(i > last)
    def _():
        o_ref[...] = jnp.zeros_like(o_ref)


def _experts(buf, block_expert, n_used, w1, w3, w2, layer):
    R, D = buf.shape
    FF = w1.shape[-1]
    nb = R // MOE_BLOCK
    blk = lambda i, be, nu: (jnp.minimum(i, nu[0] - 1), 0)
    wsel = lambda i, be, nu: (layer, be[jnp.minimum(i, nu[0] - 1)], 0, 0)
    return pl.pallas_call(
        _expert_kernel,
        grid_spec=pltpu.PrefetchScalarGridSpec(
            num_scalar_prefetch=2,
            grid=(nb,),
            in_specs=[
                pl.BlockSpec((MOE_BLOCK, D), blk),
                pl.BlockSpec((None, None, D, FF), wsel),
                pl.BlockSpec((None, None, D, FF), wsel),
                pl.BlockSpec((None, None, FF, D), wsel),
            ],
            out_specs=pl.BlockSpec((MOE_BLOCK, D), lambda i, be, nu: (i, 0)),
            scratch_shapes=[pltpu.VMEM((D, FF), BF16), pltpu.VMEM((D, FF), BF16), pltpu.VMEM((FF, D), BF16)],
        ),
        out_shape=jax.ShapeDtypeStruct((R, D), F32),
        compiler_params=_cparams(("arbitrary",)),
        name="expert_mlp",
    )(block_expert, n_used, buf, w1, w3, w2)


def _final_kernel(x_ref, r0_ref, r1_ref, rt_ref, g_ref, o_ref):
    o_ref[...] = _rms(_moe_combine(x_ref, r0_ref, r1_ref, rt_ref), g_ref[...])


def _final_norm(x2, moe, g):
    T, D = x2.shape
    tm = min(ROW_TILE, T)
    row = lambda n: pl.BlockSpec((tm, n), lambda i: (i, 0))
    return pl.pallas_call(
        _final_kernel,
        grid=(T // tm,),
        in_specs=[row(D), row(D), row(D), row(LANES), pl.BlockSpec((1, D), lambda i: (0, 0))],
        out_specs=row(D),
        out_shape=jax.ShapeDtypeStruct((T, D), F32),
        compiler_params=_cparams(("parallel",)),
        name="final_norm",
    )(x2, *moe, g)


def _pack_plan(D):
    widths = (D_ATTN, D_ATTN, D_ATTN, ATTN_HEADS, D_SSD, CONV_CH, SSD_HEADS, D, D)
    src = dict(zip(("q", "k", "v", "f", "z", "xbc", "dt", "ga", "gs"),
                   zip(np.concatenate([[0], np.cumsum(widths)[:-1]]).tolist(), widths)))
    plan, dst = [], 0
    for name in ("q", "k", "v", "z", "xbc", "ga", "gs", "f", "dt"):
        s0, w = src[name]
        plan.append((s0, w, dst, ATTN_HEAD_DIM ** -0.5 * LOG2E if name == "q" else 1.0))
        dst += w
    return plan, dst


def _pack_kernel(w_ref, o_ref, *, plan, used):
    for s0, w, d0, scale in plan:
        piece = w_ref[:, s0:s0 + w]
        if scale != 1.0:
            piece = piece * scale
        o_ref[:, d0:d0 + w] = piece.astype(BF16)
    o_ref[:, used:] = jnp.zeros((o_ref.shape[0], o_ref.shape[1] - used), BF16)


def _pack_w_in(w_in):
    depth, D, cols = w_in.shape
    plan, used = _pack_plan(D)
    n_out = -(-used // LANES) * LANES
    rows = PACK_ROWS
    return pl.pallas_call(
        functools.partial(_pack_kernel, plan=plan, used=used),
        grid=(depth, D // rows),
        in_specs=[pl.BlockSpec((None, rows, cols), lambda l, r: (l, r, 0))],
        out_specs=pl.BlockSpec((None, rows, n_out), lambda l, r: (l, r, 0)),
        out_shape=jax.ShapeDtypeStruct((depth, D, n_out), BF16),
        compiler_params=_cparams(("parallel", "parallel")),
        name="pack_w_in",
    )(w_in)


def _small_rows(mat, lane0):
    depth, n = mat.shape
    return jnp.zeros((depth, 1, SMALL_W), F32).at[:, 0, lane0:lane0 + n].set(mat.astype(F32))


def kernel(x, norm_mix_g, w_in, b_f, conv_w, conv_b, dt_bias, a_log, d_skip, ssd_norm_g, w_br_attn, w_br_ssd,
           w_out, norm_ffn_g, w_group_router, b_group_router, w_expert_router, b_expert_router, w1, w3, w2,
           final_g):
    B, S, D = x.shape
    T = B * S
    depth = w_in.shape[0]
    w_packed = _pack_w_in(w_in)
    bf_rows = _small_rows(b_f, 0)
    dtb_rows = _small_rows(dt_bias, HEAD_LANE0)
    alog_rows = _small_rows(a_log, HEAD_LANE0)
    dskip_rows = jnp.repeat(d_skip, SSD_HEAD_DIM, axis=1)[:, None, :]
    router_pad = LANES - N_EXPERT_GROUPS - N_EXPERTS
    w_router = jnp.concatenate([w_group_router, w_expert_router, jnp.zeros((depth, D, router_pad), F32)], axis=2)
    b_router = jnp.concatenate([b_group_router, b_expert_router, jnp.zeros((depth, router_pad), F32)],
                               axis=1)[:, None, :]
    wa, ws, wo = w_br_attn.astype(BF16), w_br_ssd.astype(BF16), w_out.astype(BF16)

    x2 = x.reshape(T, D)
    moe = None
    for l in range(depth):
        x2, (qkv, z, xbc, gates, small) = _inproj(x2, moe, norm_mix_g[l][None, :], w_packed, l)
        small3 = small.reshape(B, S, SMALL_W)
        fk = _fcum(small3, bf_rows[l])
        y_attn = _attention(qkv.reshape(B, S, 3 * D_ATTN), fk).reshape(T, D_ATTN)
        y_ssd = _ssd(xbc.reshape(B, S, CONV_CH), z.reshape(B, S, D_SSD), small3, conv_w[l], conv_b[l][None, :],
                     dtb_rows[l], alog_rows[l], dskip_rows[l], ssd_norm_g[l][None, :]).reshape(T, D_SSD)
        x2, h2, logits = _merge(y_attn, y_ssd, gates, x2, wa[l], ws[l], wo[l], norm_ffn_g[l][None, :],
                                w_router[l], b_router[l])
        route, cnt = _route(logits)
        dest0, dest1, src_tok, block_expert, n_used = _dispatch_plan(route, cnt)
        out = _experts(h2[src_tok], block_expert, n_used, w1, w3, w2, l)
        moe = (out[dest0], out[dest1], route)
    return _final_norm(x2, moe, final_g[None, :]).reshape(B, S, D)
```
